```python
import math
import jax, jax.numpy as jnp
from jax import lax
import numpy as np

D_MODEL = 1024
BATCH = 4
SEQ = 8192
DEPTH = 2

GRID_W = 64
CTX_LEN = 256
D_FF = 4 * D_MODEL
N_MOD = 6

MLA_HEADS = 8
QK_NOPE = 64
QK_ROPE = 32
QK_DIM = QK_NOPE + QK_ROPE
V_DIM = 64
MLA_WIDTH = MLA_HEADS * V_DIM
Q_LORA = 256
KV_LORA = 128
AXIS_DIM = QK_ROPE // 2
ROPE_THETA = 10000.0
Q_BLOCK = 128
SM_SCALE = 1.0 / math.sqrt(QK_DIM)

CONF_WIDTH = 256
CONF_K = 31
SC_WIDTH = 256
SC_K = 3

MIX_WIDTH = MLA_WIDTH + CONF_WIDTH + SC_WIDTH

Q_END = Q_LORA
KV_START = Q_END
KV_END = KV_START + KV_LORA
ROPE_END = KV_END + QK_ROPE
CONF_END = ROPE_END + 2 * CONF_WIDTH
IN_WIDTH = CONF_END + 3 * SC_WIDTH

EPS = 1e-6

kernel_name = "hybrid_mla_conformer_shortconv_dit"


def rmsnorm(u, g):
    uf = u.astype(jnp.float32)
    y = uf * lax.rsqrt(jnp.mean(uf * uf, axis=-1, keepdims=True) + EPS)
    return (y * g.astype(jnp.float32)).astype(u.dtype)


def layernorm(u, g, b):
    uf = u.astype(jnp.float32)
    mu = jnp.mean(uf, axis=-1, keepdims=True)
    d = uf - mu
    y = d * lax.rsqrt(jnp.mean(d * d, axis=-1, keepdims=True) + EPS)
    return (y * g.astype(jnp.float32) + b.astype(jnp.float32)).astype(u.dtype)


def modulation(cvec, w_mod, b_mod):
    m = jax.nn.silu(cvec) @ w_mod + b_mod
    return jnp.split(m[:, None, :], N_MOD, axis=-1)


def modulate(h, shift, scale):
    return h * (1.0 + scale) + shift


def axial_rope_tables(rows):
    row = jnp.broadcast_to(jnp.arange(rows)[:, None], (rows, GRID_W)).reshape(-1).astype(jnp.float32)
    col = jnp.broadcast_to(jnp.arange(GRID_W)[None, :], (rows, GRID_W)).reshape(-1).astype(jnp.float32)
    inv = 1.0 / (ROPE_THETA ** (jnp.arange(0, AXIS_DIM, 2, dtype=jnp.float32) / AXIS_DIM))
    ar = row[:, None] * inv
    ac = col[:, None] * inv
    ang = jnp.concatenate([ar, ar, ac, ac], axis=-1)
    return jnp.cos(ang), jnp.sin(ang)


def apply_axial_rope(u, cos, sin):
    def rot(p):
        h = p.shape[-1] // 2
        return jnp.concatenate([-p[..., h:], p[..., :h]], axis=-1)
    rotated = jnp.concatenate([rot(u[..., :AXIS_DIM]), rot(u[..., AXIS_DIM:])], axis=-1)
    return (u.astype(jnp.float32) * cos + rotated.astype(jnp.float32) * sin).astype(u.dtype)


def mla_queries(zq, g_q, w_q_b, rope):
    b, s, _ = zq.shape
    q = (rmsnorm(zq, g_q) @ w_q_b).reshape(b, s, MLA_HEADS, QK_DIM)
    if rope is None:
        return q
    cos, sin = rope
    q_rope = apply_axial_rope(q[..., QK_NOPE:], cos[:, None, :], sin[:, None, :])
    return jnp.concatenate([q[..., :QK_NOPE], q_rope], axis=-1)


def mla_keys_values(zkv, g_kv, w_kv_b, rope):
    b, s, _ = zkv.shape
    ckv, k_rope = zkv[..., :KV_LORA], zkv[..., KV_LORA:]
    kv = (rmsnorm(ckv, g_kv) @ w_kv_b).reshape(b, s, MLA_HEADS, QK_NOPE + V_DIM)
    k_nope, v = kv[..., :QK_NOPE], kv[..., QK_NOPE:]
    if rope is not None:
        cos, sin = rope
        k_rope = apply_axial_rope(k_rope, cos, sin)
    k_rope = jnp.broadcast_to(k_rope[:, :, None, :], (b, s, MLA_HEADS, QK_ROPE))
    return jnp.concatenate([k_nope, k_rope], axis=-1), v


def block_attention(q, k, v):
    b, s, h, dq = q.shape
    dv = v.shape[-1]
    nb = s // Q_BLOCK
    qb = jnp.moveaxis(q.reshape(b, nb, Q_BLOCK, h, dq), 1, 0)

    def attend(qblk):
        logits = jnp.einsum('bqhd,bkhd->bhqk', qblk, k).astype(jnp.float32) * SM_SCALE
        p = jax.nn.softmax(logits, axis=-1).astype(v.dtype)
        return jnp.einsum('bhqk,bkhd->bqhd', p, v)

    o = lax.map(attend, qb)
    return jnp.moveaxis(o, 0, 1).reshape(b, s, h * dv)


def depthwise_conv(u, w):
    k, ch = w.shape
    return lax.conv_general_dilated(
        u, w[:, None, :].astype(u.dtype), window_strides=(1,),
        padding=[(k // 2, k // 2)], dimension_numbers=('NWC', 'WIO', 'NWC'),
        feature_group_count=ch)


def conformer_branch(z, dw_w, dw_b, ln_g, ln_b):
    a, g = jnp.split(z, 2, axis=-1)
    u = a * jax.nn.sigmoid(g)
    u = depthwise_conv(u, dw_w) + dw_b
    u = layernorm(u, ln_g, ln_b)
    return jax.nn.silu(u)


def shortconv_branch(z, w):
    b_gate, c_gate, h = jnp.split(z, 3, axis=-1)
    return b_gate * depthwise_conv(c_gate * h, w)


def mix_stream(z, k, v, rope, g_q, w_q_b, conf_dw_w, conf_dw_b, conf_ln_g, conf_ln_b,
               sc_dw_w, g_branch, w_o):
    q = mla_queries(z[..., :Q_END], g_q, w_q_b, rope)
    attn = block_attention(q, k, v)
    conf = conformer_branch(z[..., ROPE_END:CONF_END], conf_dw_w, conf_dw_b, conf_ln_g, conf_ln_b)
    sc = shortconv_branch(z[..., CONF_END:], sc_dw_w)
    merged = jnp.concatenate([
        rmsnorm(attn, g_branch[:MLA_WIDTH]),
        rmsnorm(conf, g_branch[MLA_WIDTH:MLA_WIDTH + CONF_WIDTH]),
        rmsnorm(sc, g_branch[MLA_WIDTH + CONF_WIDTH:])], axis=-1)
    return merged @ w_o


def sqrelu_mlp(h, w1, w2):
    a = jax.nn.relu(h @ w1)
    return (a * a) @ w2


def setup_inputs(seed: int = 0) -> dict:
    key = jax.random.key(seed)
    ks = jax.random.split(key, 32)
    f32 = jnp.float32

    def nrm(k, shape, scale):
        return jax.random.normal(k, shape, f32) * scale

    def gain(k, shape):
        return 1.0 + 0.05 * jax.random.normal(k, shape, f32)

    L, D = DEPTH, D_MODEL
    return {
        "x": nrm(ks[0], (BATCH, SEQ, D), 1.0),
        "c": nrm(ks[1], (BATCH, D), 1.0),
        "ctx": nrm(ks[2], (BATCH, CTX_LEN, D), 1.0),
        "c_ctx": nrm(ks[3], (D,), 1.0),
        "w_mod": nrm(ks[4], (L, D, N_MOD * D), 0.5 * D ** -0.5),
        "b_mod": nrm(ks[5], (L, N_MOD * D), 0.02),
        "g_pre_mix": gain(ks[6], (L, D)),
        "g_post_mix": gain(ks[7], (L, D)),
        "g_pre_mlp": gain(ks[8], (L, D)),
        "g_post_mlp": gain(ks[9], (L, D)),
        "w_in": nrm(ks[10], (L, D, IN_WIDTH), D ** -0.5),
        "g_q": gain(ks[11], (L, Q_LORA)),
        "w_q_b": nrm(ks[12], (L, Q_LORA, MLA_HEADS * QK_DIM), Q_LORA ** -0.5),
        "g_kv": gain(ks[13], (L, KV_LORA)),
        "w_kv_b": nrm(ks[14], (L, KV_LORA, MLA_HEADS * (QK_NOPE + V_DIM)), KV_LORA ** -0.5),
        "conf_dw_w": nrm(ks[15], (L, CONF_K, CONF_WIDTH), CONF_K ** -0.5),
        "conf_dw_b": nrm(ks[16], (L, CONF_WIDTH), 0.02),
        "conf_ln_g": gain(ks[17], (L, CONF_WIDTH)),
        "conf_ln_b": nrm(ks[18], (L, CONF_WIDTH), 0.02),
        "sc_dw_w": nrm(ks[19], (L, SC_K, SC_WIDTH), SC_K ** -0.5),
        "g_branch": gain(ks[20], (L, MIX_WIDTH)),
        "w_o": nrm(ks[21], (L, MIX_WIDTH, D), MIX_WIDTH ** -0.5),
        "w_mlp_in": nrm(ks[22], (L, D, D_FF), D ** -0.5),
        "w_mlp_out": nrm(ks[23], (L, D_FF, D), D_FF ** -0.5),
    }


def reference(x, c, ctx, c_ctx, w_mod, b_mod, g_pre_mix, g_post_mix, g_pre_mlp, g_post_mlp,
              w_in, g_q, w_q_b, g_kv, w_kv_b, conf_dw_w, conf_dw_b, conf_ln_g, conf_ln_b,
              sc_dw_w, g_branch, w_o, w_mlp_in, w_mlp_out):
    n_lat = x.shape[1]
    rows = n_lat // GRID_W
    rope = axial_rope_tables(rows)
    xl, xc = x, ctx
    for i in range(DEPTH):
        last = i == DEPTH - 1
        ml = modulation(c, w_mod[i], b_mod[i])
        mc = modulation(c_ctx[None, :], w_mod[i], b_mod[i])

        hl = modulate(rmsnorm(xl, g_pre_mix[i]), ml[0], ml[1])
        hc = modulate(rmsnorm(xc, g_pre_mix[i]), mc[0], mc[1])
        zl = hl @ w_in[i]
        if last:
            zc_kv = hc @ w_in[i][:, KV_START:ROPE_END]
        else:
            zc = hc @ w_in[i]
            zc_kv = zc[..., KV_START:ROPE_END]

        k_c, v_c = mla_keys_values(zc_kv, g_kv[i], w_kv_b[i], None)
        k_l, v_l = mla_keys_values(zl[..., KV_START:ROPE_END], g_kv[i], w_kv_b[i], rope)
        y_l = mix_stream(zl, jnp.concatenate([k_c, k_l], axis=1), jnp.concatenate([v_c, v_l], axis=1),
                         rope, g_q[i], w_q_b[i], conf_dw_w[i], conf_dw_b[i], conf_ln_g[i],
                         conf_ln_b[i], sc_dw_w[i], g_branch[i], w_o[i])
        xl_new = xl + ml[2] * rmsnorm(y_l, g_post_mix[i])

        h2 = modulate(rmsnorm(xl_new, g_pre_mlp[i]), ml[3], ml[4])
        xl_new = xl_new + ml[5] * rmsnorm(sqrelu_mlp(h2, w_mlp_in[i], w_mlp_out[i]), g_post_mlp[i])

        if not last:
            y_c = mix_stream(zc, k_c, v_c, None, g_q[i], w_q_b[i], conf_dw_w[i], conf_dw_b[i],
                             conf_ln_g[i], conf_ln_b[i], sc_dw_w[i], g_branch[i], w_o[i])
            xc_new = xc + mc[2] * rmsnorm(y_c, g_post_mix[i])
            h2c = modulate(rmsnorm(xc_new, g_pre_mlp[i]), mc[3], mc[4])
            xc = xc_new + mc[5] * rmsnorm(sqrelu_mlp(h2c, w_mlp_in[i], w_mlp_out[i]), g_post_mlp[i])
        xl = xl_new
    return xl
```

```python
import functools
import math

import jax
import jax.numpy as jnp
from jax import lax
from jax.experimental import pallas as pl
from jax.experimental.pallas import tpu as pltpu

F32 = jnp.float32
BF16 = jnp.bfloat16

D_MODEL = 1024
GRID_W = 64
N_MOD = 6
MLA_HEADS = 8
QK_NOPE = 64
QK_ROPE = 32
QK_DIM = QK_NOPE + QK_ROPE
V_DIM = 64
MLA_WIDTH = MLA_HEADS * V_DIM
Q_LORA = 256
KV_LORA = 128
AXIS_DIM = QK_ROPE // 2
ROPE_THETA = 10000.0
CONF_WIDTH = 256
CONF_K = 31
SC_WIDTH = 256
SC_K = 3
D_FF = 4 * D_MODEL
EPS = 1e-6
SM_SCALE = 1.0 / math.sqrt(QK_DIM)

LANES = 128
SUBLANES = 8
BF16_ROWS = 16

HEAD_PAD = LANES
V_ROWS = V_DIM + BF16_ROWS
HALO = 16

C_Q = 0
C_CKV = C_Q + Q_LORA
C_KR = C_CKV + KV_LORA
C_KR_ROT = C_KR + HEAD_PAD
C_CONF_A = C_KR_ROT + HEAD_PAD
C_CONF_G = C_CONF_A + CONF_WIDTH
C_SC_B = C_CONF_G + CONF_WIDTH
C_SC_C = C_SC_B + SC_WIDTH
C_SC_H = C_SC_C + SC_WIDTH
IN_PACKED = C_SC_H + SC_WIDTH

NEG_BIG = -1e30
VMEM_LIMIT = 56 * 1024 * 1024


def _const_spec(shape):
    nd = len(shape)
    return pl.BlockSpec(shape, lambda *_: (0,) * nd)


def _rms_scale(v, width):
    return lax.rsqrt(jnp.sum(v * v, axis=-1, keepdims=True) * (1.0 / width) + EPS)


def _mod_kernel(c_ref, w_ref, b_ref, o_ref):
    cv = c_ref[...]
    s = cv * jax.nn.sigmoid(cv)
    o_ref[0] = jnp.dot(s, w_ref[0], precision=lax.Precision.HIGHEST,
                       preferred_element_type=F32) + b_ref[0]


def _modulation(cvecs, w_mod, b_mod):
    depth = w_mod.shape[0]
    rows = cvecs.shape[0]
    return pl.pallas_call(
        _mod_kernel,
        grid=(depth, N_MOD),
        in_specs=[
            _const_spec((rows, D_MODEL)),
            pl.BlockSpec((1, D_MODEL, D_MODEL), lambda l, j: (l, 0, j)),
            pl.BlockSpec((1, 1, D_MODEL), lambda l, j: (l, 0, j)),
        ],
        out_specs=pl.BlockSpec((1, rows, D_MODEL), lambda l, j: (l, 0, j)),
        out_shape=jax.ShapeDtypeStruct((depth, rows, N_MOD * D_MODEL), F32),
        compiler_params=pltpu.CompilerParams(
            dimension_semantics=("arbitrary", "arbitrary"), vmem_limit_bytes=VMEM_LIMIT),
        name="modulation",
    )(cvecs, w_mod, b_mod.reshape(depth, 1, N_MOD * D_MODEL))


def _in_proj_kernel(x_ref, shift_ref, scale_ref, gpre_ref, win_ref, gq_ref, wq_ref, gkv_ref,
                    wk_ref, wvt_ref, cos_ref, sin_ref, q_ref, k_ref, vt_ref, cv_ref):
    x = x_ref[0]
    gain = gpre_ref[...] * (1.0 + scale_ref[0])
    h = (x * _rms_scale(x, D_MODEL)) * gain + shift_ref[0]
    z = jnp.dot(h.astype(BF16), win_ref[...], preferred_element_type=F32)
    cos = cos_ref[...]
    sin = sin_ref[...]

    zq = z[:, C_Q:C_Q + Q_LORA]
    zqn = (zq * _rms_scale(zq, Q_LORA)) * gq_ref[...]
    qq = jnp.dot(zqn.astype(BF16), wq_ref[...], preferred_element_type=F32)
    half = MLA_HEADS * HEAD_PAD
    for hd in range(MLA_HEADS):
        lo = hd * HEAD_PAD
        qh = qq[:, lo:lo + HEAD_PAD] * cos + qq[:, half + lo:half + lo + HEAD_PAD] * sin
        q_ref[0, :, lo:lo + HEAD_PAD] = qh.astype(BF16)

    ckv = z[:, C_CKV:C_CKV + KV_LORA]
    ckvn = ((ckv * _rms_scale(ckv, KV_LORA)) * gkv_ref[...]).astype(BF16)
    kk = jnp.dot(ckvn, wk_ref[...], preferred_element_type=F32)
    kr = z[:, C_KR:C_KR + HEAD_PAD] * cos + z[:, C_KR_ROT:C_KR_ROT + HEAD_PAD] * sin
    for hd in range(MLA_HEADS):
        lo = hd * HEAD_PAD
        k_ref[0, :, lo:lo + HEAD_PAD] = (kk[:, lo:lo + HEAD_PAD] + kr).astype(BF16)

    vt = lax.dot_general(wvt_ref[...], ckvn, (((1,), (1,)), ((), ())),
                         preferred_element_type=F32)
    ones = jnp.ones((V_ROWS - V_DIM, vt.shape[1]), BF16)
    for hd in range(MLA_HEADS):
        vt_ref[0, hd * V_ROWS:hd * V_ROWS + V_DIM, :] = (
            vt[hd * V_DIM:(hd + 1) * V_DIM].astype(BF16))
        vt_ref[0, hd * V_ROWS + V_DIM:(hd + 1) * V_ROWS, :] = ones

    a = z[:, C_CONF_A:C_CONF_A + CONF_WIDTH]
    g = z[:, C_CONF_G:C_CONF_G + CONF_WIDTH]
    cv_ref[0, :, 0:CONF_WIDTH] = a * jax.nn.sigmoid(g)
    cv_ref[0, :, CONF_WIDTH:CONF_WIDTH + SC_WIDTH] = (
        z[:, C_SC_C:C_SC_C + SC_WIDTH] * z[:, C_SC_H:C_SC_H + SC_WIDTH])
    cv_ref[0, :, CONF_WIDTH + SC_WIDTH:] = z[:, C_SC_B:C_SC_B + SC_WIDTH]


def _in_proj(x, shift, scale, wts, cos_t, sin_t, tm):
    bsz, seq, _ = x.shape
    n_t = seq // tm
    tok = lambda b, i: (b, i, 0)
    vec = lambda b, i: (b, 0, 0)
    cw = CONF_WIDTH + 2 * SC_WIDTH
    return pl.pallas_call(
        _in_proj_kernel,
        grid=(bsz, n_t),
        in_specs=[
            pl.BlockSpec((1, tm, D_MODEL), tok),
            pl.BlockSpec((1, 1, D_MODEL), vec),
            pl.BlockSpec((1, 1, D_MODEL), vec),
            _const_spec((1, D_MODEL)),
            _const_spec((D_MODEL, IN_PACKED)),
            _const_spec((1, Q_LORA)),
            _const_spec((Q_LORA, 2 * MLA_HEADS * HEAD_PAD)),
            _const_spec((1, KV_LORA)),
            _const_spec((KV_LORA, MLA_HEADS * HEAD_PAD)),
            _const_spec((MLA_WIDTH, KV_LORA)),
            pl.BlockSpec((tm, HEAD_PAD), lambda b, i: (i, 0)),
            pl.BlockSpec((tm, HEAD_PAD), lambda b, i: (i, 0)),
        ],
        out_specs=[
            pl.BlockSpec((1, tm, MLA_HEADS * HEAD_PAD), tok),
            pl.BlockSpec((1, tm, MLA_HEADS * HEAD_PAD), tok),
            pl.BlockSpec((1, MLA_HEADS * V_ROWS, tm), lambda b, i: (b, 0, i)),
            pl.BlockSpec((1, tm, cw), tok),
        ],
        out_shape=[
            jax.ShapeDtypeStruct((bsz, seq, MLA_HEADS * HEAD_PAD), BF16),
            jax.ShapeDtypeStruct((bsz, seq, MLA_HEADS * HEAD_PAD), BF16),
            jax.ShapeDtypeStruct((bsz, MLA_HEADS * V_ROWS, seq), BF16),
            jax.ShapeDtypeStruct((bsz, seq, cw), F32),
        ],
        compiler_params=pltpu.CompilerParams(
            dimension_semantics=("arbitrary", "arbitrary"), vmem_limit_bytes=VMEM_LIMIT),
        name="in_proj",
    )(x, shift, scale, wts["g_pre_mix"], wts["w_in"], wts["g_q"], wts["w_q"], wts["g_kv"],
      wts["w_k"], wts["w_vt"], cos_t, sin_t)


def _attn_kernel(*refs, chunks):
    q_ref = refs[0]
    o_ref = refs[-1]
    q = q_ref[0]
    tq = q.shape[0]
    m = jnp.full((1, tq), NEG_BIG, F32)
    acc = jnp.zeros((V_ROWS, tq), F32)

    for src, (n_chunks, tk) in enumerate(chunks):
        k_ref = refs[1 + 2 * src]
        vt_ref = refs[2 + 2 * src]

        def body(c, carry, k_ref=k_ref, vt_ref=vt_ref, tk=tk):
            m, acc = carry
            start = pl.multiple_of(c * tk, tk)
            kc = k_ref[0, pl.ds(start, tk), :]
            s = lax.dot_general(kc, q, (((1,), (1,)), ((), ())),
                                preferred_element_type=F32)
            m_new = jnp.maximum(m, jnp.max(s, axis=0, keepdims=True))
            alpha = jnp.exp2(m - m_new)
            p = jnp.exp2(s - m_new).astype(BF16)
            vc = vt_ref[0, :, pl.ds(start, tk)]
            acc = acc * alpha + jnp.dot(vc, p, preferred_element_type=F32)
            return m_new, acc

        if n_chunks == 1:
            m, acc = body(0, (m, acc))
        else:
            m, acc = lax.fori_loop(0, n_chunks, body, (m, acc))

    o_ref[0] = acc[0:V_DIM] * (1.0 / acc[V_DIM:V_DIM + 1])


def _attention(q, kv_sources, tq):
    bsz, seq, _ = q.shape
    in_specs = [pl.BlockSpec((1, tq, HEAD_PAD), lambda b, h, i: (b, i, h))]
    args = [q]
    chunks = []
    for k, vt, tk in kv_sources:
        t = k.shape[1]
        in_specs.append(pl.BlockSpec((1, t, HEAD_PAD), lambda b, h, i: (b, 0, h)))
        in_specs.append(pl.BlockSpec((1, V_ROWS, t), lambda b, h, i: (b, h, 0)))
        args += [k, vt]
        chunks.append((t // tk, tk))
    return pl.pallas_call(
        functools.partial(_attn_kernel, chunks=tuple(chunks)),
        grid=(bsz, MLA_HEADS, seq // tq),
        in_specs=in_specs,
        out_specs=pl.BlockSpec((1, V_DIM, tq), lambda b, h, i: (b, h, i)),
        out_shape=jax.ShapeDtypeStruct((bsz, MLA_WIDTH, seq), F32),
        compiler_params=pltpu.CompilerParams(
            dimension_semantics=("arbitrary", "arbitrary", "arbitrary"),
            vmem_limit_bytes=VMEM_LIMIT),
        name="attention",
    )(*args)


def _mix_out_kernel(x_ref, ot_ref, cv_ref, prev_ref, next_ref, gate_ref, gbr_ref, wo_ref,
                    cw_ref, cb_ref, lng_ref, lnb_ref, sw_ref, gpost_ref, o_ref,
                    uext_ref, cext_ref):
    tm = x_ref.shape[1]
    i = pl.program_id(1)
    has_prev = (i > 0).astype(F32)
    has_next = (i < pl.num_programs(1) - 1).astype(F32)
    c0, c1, c2 = 0, CONF_WIDTH, CONF_WIDTH + SC_WIDTH

    a = ot_ref[0].T
    an = (a * _rms_scale(a, MLA_WIDTH)) * gbr_ref[:, 0:MLA_WIDTH]
    y = jnp.dot(an.astype(BF16), wo_ref[0:MLA_WIDTH, :], preferred_element_type=F32)

    uext_ref[0:HALO, :] = prev_ref[0, :, c0:c1] * has_prev
    uext_ref[HALO:HALO + tm, :] = cv_ref[0, :, c0:c1]
    uext_ref[HALO + tm:, :] = next_ref[0, :, c0:c1] * has_next
    base = HALO - CONF_K // 2
    conv = cb_ref[...] + cw_ref[0:1, :] * uext_ref[base:base + tm, :]
    for t in range(1, CONF_K):
        conv = conv + cw_ref[t:t + 1, :] * uext_ref[base + t:base + t + tm, :]
    mu = jnp.sum(conv, axis=-1, keepdims=True) * (1.0 / CONF_WIDTH)
    d = conv - mu
    ln = (d * _rms_scale(d, CONF_WIDTH)) * lng_ref[...] + lnb_ref[...]
    cf = ln * jax.nn.sigmoid(ln)
    cfn = (cf * _rms_scale(cf, CONF_WIDTH)) * gbr_ref[:, MLA_WIDTH:MLA_WIDTH + CONF_WIDTH]
    y = y + jnp.dot(cfn.astype(BF16), wo_ref[MLA_WIDTH:MLA_WIDTH + CONF_WIDTH, :],
                    preferred_element_type=F32)

    cext_ref[0:SUBLANES, :] = prev_ref[0, HALO - SUBLANES:, c1:c2] * has_prev
    cext_ref[SUBLANES:SUBLANES + tm, :] = cv_ref[0, :, c1:c2]
    cext_ref[SUBLANES + tm:, :] = next_ref[0, 0:SUBLANES, c1:c2] * has_next
    base = SUBLANES - SC_K // 2
    sc = sw_ref[0:1, :] * cext_ref[base:base + tm, :]
    for t in range(1, SC_K):
        sc = sc + sw_ref[t:t + 1, :] * cext_ref[base + t:base + t + tm, :]
    sc = cv_ref[0, :, c2:] * sc
    scn = (sc * _rms_scale(sc, SC_WIDTH)) * gbr_ref[:, MLA_WIDTH + CONF_WIDTH:]
    y = y + jnp.dot(scn.astype(BF16), wo_ref[MLA_WIDTH + CONF_WIDTH:, :],
                    preferred_element_type=F32)

    yn = (y * _rms_scale(y, D_MODEL)) * gpost_ref[...]
    o_ref[0] = x_ref[0] + gate_ref[0] * yn


def _mix_out(x, ot, cv, gate, wts, tm):
    bsz, seq, _ = x.shape
    n_t = seq // tm
    hb = tm // HALO
    n_h = seq // HALO
    cw = cv.shape[-1]
    tok = lambda b, i: (b, i, 0)
    return pl.pallas_call(
        _mix_out_kernel,
        grid=(bsz, n_t),
        in_specs=[
            pl.BlockSpec((1, tm, D_MODEL), tok),
            pl.BlockSpec((1, MLA_WIDTH, tm), lambda b, i: (b, 0, i)),
            pl.BlockSpec((1, tm, cw), tok),
            pl.BlockSpec((1, HALO, cw), lambda b, i: (b, jnp.maximum(i * hb - 1, 0), 0)),
            pl.BlockSpec((1, HALO, cw), lambda b, i: (b, jnp.minimum((i + 1) * hb, n_h - 1), 0)),
            pl.BlockSpec((1, 1, D_MODEL), lambda b, i: (b, 0, 0)),
            _const_spec((1, D_MODEL)),
            _const_spec((D_MODEL, D_MODEL)),
            _const_spec((CONF_K, CONF_WIDTH)),
            _const_spec((1, CONF_WIDTH)),
            _const_spec((1, CONF_WIDTH)),
            _const_spec((1, CONF_WIDTH)),
            _const_spec((SC_K, SC_WIDTH)),
            _const_spec((1, D_MODEL)),
        ],
        out_specs=pl.BlockSpec((1, tm, D_MODEL), tok),
        out_shape=jax.ShapeDtypeStruct((bsz, seq, D_MODEL), F32),
        scratch_shapes=[
            pltpu.VMEM((tm + 2 * HALO, CONF_WIDTH), F32),
            pltpu.VMEM((tm + 2 * SUBLANES, SC_WIDTH), F32),
        ],
        compiler_params=pltpu.CompilerParams(
            dimension_semantics=("arbitrary", "arbitrary"), vmem_limit_bytes=VMEM_LIMIT),
        name="mix_out",
    )(x, ot, cv, cv, cv, gate, wts["g_branch"], wts["w_o"], wts["conf_dw_w"], wts["conf_dw_b"],
      wts["conf_ln_g"], wts["conf_ln_b"], wts["sc_dw_w"], wts["g_post_mix"])


def _mlp_kernel(x_ref, shift_ref, scale_ref, gate_ref, gpre_ref, w1_ref, w2_ref, gpost_ref,
                o_ref, *, ff_chunk):
    x = x_ref[0]
    gain = gpre_ref[...] * (1.0 + scale_ref[0])
    h = ((x * _rms_scale(x, D_MODEL)) * gain + shift_ref[0]).astype(BF16)
    y = None
    for j in range(D_FF // ff_chunk):
        lo = j * ff_chunk
        a = jnp.maximum(jnp.dot(h, w1_ref[:, lo:lo + ff_chunk], preferred_element_type=F32), 0.0)
        part = jnp.dot((a * a).astype(BF16), w2_ref[lo:lo + ff_chunk, :],
                       preferred_element_type=F32)
        y = part if y is None else y + part
    yn = (y * _rms_scale(y, D_MODEL)) * gpost_ref[...]
    o_ref[0] = x + gate_ref[0] * yn


def _mlp(x, shift, scale, gate, wts, tm, ff_chunk=1024):
    bsz, seq, _ = x.shape
    tok = lambda b, i: (b, i, 0)
    vec = lambda b, i: (b, 0, 0)
    return pl.pallas_call(
        functools.partial(_mlp_kernel, ff_chunk=ff_chunk),
        grid=(bsz, seq // tm),
        in_specs=[
            pl.BlockSpec((1, tm, D_MODEL), tok),
            pl.BlockSpec((1, 1, D_MODEL), vec),
            pl.BlockSpec((1, 1, D_MODEL), vec),
            pl.BlockSpec((1, 1, D_MODEL), vec),
            _const_spec((1, D_MODEL)),
            pl.BlockSpec((D_MODEL, D_FF), lambda b, i: (0, 0), pipeline_mode=pl.Buffered(1)),
            pl.BlockSpec((D_FF, D_MODEL), lambda b, i: (0, 0), pipeline_mode=pl.Buffered(1)),
            _const_spec((1, D_MODEL)),
        ],
        out_specs=pl.BlockSpec((1, tm, D_MODEL), tok),
        out_shape=jax.ShapeDtypeStruct((bsz, seq, D_MODEL), F32),
        compiler_params=pltpu.CompilerParams(
            dimension_semantics=("arbitrary", "arbitrary"), vmem_limit_bytes=VMEM_LIMIT),
        name="mlp",
    )(x, shift, scale, gate, wts["g_pre_mlp"], wts["w_mlp_in"], wts["w_mlp_out"],
      wts["g_post_mlp"])


def _rotate_half_cols(w):
    hh = AXIS_DIM // 2
    parts = []
    for ax in range(2):
        blk = w[..., ax * AXIS_DIM:(ax + 1) * AXIS_DIM]
        parts.append(jnp.concatenate([-blk[..., hh:], blk[..., :hh]], axis=-1))
    return jnp.concatenate(parts, axis=-1)


def _head_group(nope, rope):
    pad = jnp.zeros(nope.shape[:-1] + (HEAD_PAD - QK_DIM,), nope.dtype)
    return jnp.concatenate([nope, rope, pad], axis=-1)


def _pack_layer(i, w_in, g_q, w_q_b, g_kv, w_kv_b, p):
    win = w_in[i]
    zeros_nope = jnp.zeros((D_MODEL, QK_NOPE), F32)
    w_kr = win[:, Q_LORA + KV_LORA:Q_LORA + KV_LORA + QK_ROPE]
    conf0 = Q_LORA + KV_LORA + QK_ROPE
    sc0 = conf0 + 2 * CONF_WIDTH
    win_packed = jnp.concatenate([
        win[:, 0:Q_LORA],
        win[:, Q_LORA:Q_LORA + KV_LORA],
        _head_group(zeros_nope, w_kr),
        _head_group(zeros_nope, _rotate_half_cols(w_kr)),
        win[:, conf0:sc0],
        win[:, sc0:],
    ], axis=-1)

    wq = w_q_b[i].reshape(Q_LORA, MLA_HEADS, QK_DIM)
    wq_nope, wq_rope = wq[..., :QK_NOPE], wq[..., QK_NOPE:]
    wq_main = _head_group(wq_nope, wq_rope).reshape(Q_LORA, MLA_HEADS * HEAD_PAD)
    wq_rot = _head_group(jnp.zeros_like(wq_nope), _rotate_half_cols(wq_rope)).reshape(
        Q_LORA, MLA_HEADS * HEAD_PAD)

    wkv = w_kv_b[i].reshape(KV_LORA, MLA_HEADS, QK_NOPE + V_DIM)
    wk = _head_group(wkv[..., :QK_NOPE], jnp.zeros((KV_LORA, MLA_HEADS, QK_ROPE), F32)).reshape(
        KV_LORA, MLA_HEADS * HEAD_PAD)
    wv_t = jnp.transpose(wkv[..., QK_NOPE:], (1, 2, 0))

    row = lambda v: v[i][None, :]
    return {
        "g_pre_mix": row(p["g_pre_mix"]),
        "w_in": win_packed.astype(BF16),
        "g_q": row(g_q) * (SM_SCALE * math.log2(math.e)),
        "w_q": jnp.concatenate([wq_main, wq_rot], axis=-1).astype(BF16),
        "g_kv": row(g_kv),
        "w_k": wk.astype(BF16),
        "w_vt": wv_t.reshape(MLA_WIDTH, KV_LORA).astype(BF16),
        "g_branch": row(p["g_branch"]),
        "w_o": p["w_o"][i].astype(BF16),
        "conf_dw_w": p["conf_dw_w"][i],
        "conf_dw_b": row(p["conf_dw_b"]),
        "conf_ln_g": row(p["conf_ln_g"]),
        "conf_ln_b": row(p["conf_ln_b"]),
        "sc_dw_w": p["sc_dw_w"][i],
        "g_post_mix": row(p["g_post_mix"]),
        "g_pre_mlp": row(p["g_pre_mlp"]),
        "w_mlp_in": p["w_mlp_in"][i].astype(BF16),
        "w_mlp_out": p["w_mlp_out"][i].astype(BF16),
        "g_post_mlp": row(p["g_post_mlp"]),
    }


def _rope_tables(n_lat):
    rows = n_lat // GRID_W
    row = jnp.broadcast_to(jnp.arange(rows)[:, None], (rows, GRID_W)).reshape(-1).astype(F32)
    col = jnp.broadcast_to(jnp.arange(GRID_W)[None, :], (rows, GRID_W)).reshape(-1).astype(F32)
    inv = 1.0 / (ROPE_THETA ** (jnp.arange(0, AXIS_DIM, 2, dtype=F32) / AXIS_DIM))
    ar = row[:, None] * inv
    ac = col[:, None] * inv
    ang = jnp.concatenate([ar, ar, ac, ac], axis=-1)
    return (_head_group(jnp.ones((n_lat, QK_NOPE), F32), jnp.cos(ang)),
            _head_group(jnp.zeros((n_lat, QK_NOPE), F32), jnp.sin(ang)))


def _identity_tables(n):
    return (_head_group(jnp.ones((n, QK_NOPE), F32), jnp.ones((n, QK_ROPE), F32)),
            jnp.zeros((n, HEAD_PAD), F32))


def kernel(x, c, ctx, c_ctx, w_mod, b_mod, g_pre_mix, g_post_mix, g_pre_mlp, g_post_mlp, w_in,
           g_q, w_q_b, g_kv, w_kv_b, conf_dw_w, conf_dw_b, conf_ln_g, conf_ln_b, sc_dw_w,
           g_branch, w_o, w_mlp_in, w_mlp_out):
    bsz, n_lat, _ = x.shape
    n_ctx = ctx.shape[1]
    depth = w_mod.shape[0]
    params = dict(g_pre_mix=g_pre_mix, g_post_mix=g_post_mix, g_pre_mlp=g_pre_mlp,
                  g_post_mlp=g_post_mlp, conf_dw_w=conf_dw_w, conf_dw_b=conf_dw_b,
                  conf_ln_g=conf_ln_g, conf_ln_b=conf_ln_b, sc_dw_w=sc_dw_w, g_branch=g_branch,
                  w_o=w_o, w_mlp_in=w_mlp_in, w_mlp_out=w_mlp_out)

    mod_rows = SUBLANES * pl.cdiv(bsz + 1, SUBLANES)
    cvecs = jnp.concatenate(
        [c, c_ctx[None, :], jnp.zeros((mod_rows - bsz - 1, D_MODEL), F32)], axis=0)
    mod = _modulation(cvecs, w_mod, b_mod).reshape(depth, mod_rows, N_MOD, D_MODEL)

    rope_lat = _rope_tables(n_lat)
    rope_ctx = _identity_tables(n_ctx)
    tm_lat, tm_ctx = 256, min(256, n_ctx)
    tq_lat, tq_ctx = 256, min(256, n_ctx)
    tk_lat, tk_ctx = 512, min(256, n_ctx)

    xl, xc = x, ctx
    for i in range(depth):
        last = i == depth - 1
        wts = _pack_layer(i, w_in, g_q, w_q_b, g_kv, w_kv_b, params)
        ml = [mod[i, :bsz, j][:, None, :] for j in range(N_MOD)]
        mc = [jnp.broadcast_to(mod[i, bsz, j][None, None, :], (bsz, 1, D_MODEL))
              for j in range(N_MOD)]

        q_c, k_c, vt_c, cv_c = _in_proj(xc, mc[0], mc[1], wts, *rope_ctx, tm_ctx)
        q_l, k_l, vt_l, cv_l = _in_proj(xl, ml[0], ml[1], wts, *rope_lat, tm_lat)

        ot_l = _attention(q_l, [(k_c, vt_c, tk_ctx), (k_l, vt_l, tk_lat)], tq_lat)
        xl = _mix_out(xl, ot_l, cv_l, ml[2], wts, tm_lat)
        xl = _mlp(xl, ml[3], ml[4], ml[5], wts, 512)

        if not last:
            ot_c = _attention(q_c, [(k_c, vt_c, tk_ctx)], tq_ctx)
            xc = _mix_out(xc, ot_c, cv_c, mc[2], wts, tm_ctx)
            xc = _mlp(xc, mc[3], mc[4], mc[5], wts, min(256, n_ctx))
    return xl
```

```python
import functools
import math

import jax
import jax.numpy as jnp
from jax import lax
from jax.experimental import pallas as pl
from jax.experimental.pallas import tpu as pltpu

F32 = jnp.float32
BF16 = jnp.bfloat16

D_MODEL = 1024
GRID_W = 64
N_MOD = 6
MLA_HEADS = 8
QK_NOPE = 64
QK_ROPE = 32
QK_DIM = QK_NOPE + QK_ROPE
V_DIM = 64
MLA_WIDTH = MLA_HEADS * V_DIM
Q_LORA = 256
KV_LORA = 128
AXIS_DIM = QK_ROPE // 2
ROPE_THETA = 10000.0
CONF_WIDTH = 256
CONF_K = 31
SC_WIDTH = 256
SC_K = 3
D_FF = 4 * D_MODEL
EPS = 1e-6
SM_SCALE = 1.0 / math.sqrt(QK_DIM)

LANES = 128
SUBLANES = 8
BF16_ROWS = 16

HEAD_PAD = LANES
V_ROWS = V_DIM + BF16_ROWS
HALO = 16

C_Q = 0
C_CKV = C_Q + Q_LORA
C_KR = C_CKV + KV_LORA
C_KR_ROT = C_KR + HEAD_PAD
C_CONF_A = C_KR_ROT + HEAD_PAD
C_CONF_G = C_CONF_A + CONF_WIDTH
C_SC_B = C_CONF_G + CONF_WIDTH
C_SC_C = C_SC_B + SC_WIDTH
C_SC_H = C_SC_C + SC_WIDTH
IN_PACKED = C_SC_H + SC_WIDTH

NEG_BIG = -1e30
VMEM_LIMIT = 56 * 1024 * 1024


def _const_spec(shape):
    nd = len(shape)
    return pl.BlockSpec(shape, lambda *_: (0,) * nd)


def _rms_scale(v, width):
    return lax.rsqrt(jnp.sum(v * v, axis=-1, keepdims=True) * (1.0 / width) + EPS)


def _mod_kernel(c_ref, w_ref, b_ref, o_ref):
    cv = c_ref[...]
    s = cv * jax.nn.sigmoid(cv)
    o_ref[0] = jnp.dot(s, w_ref[0], precision=lax.Precision.HIGHEST,
                       preferred_element_type=F32) + b_ref[0]


def _modulation(cvecs, w_mod, b_mod):
    depth = w_mod.shape[0]
    rows = cvecs.shape[0]
    return pl.pallas_call(
        _mod_kernel,
        grid=(depth, N_MOD),
        in_specs=[
            _const_spec((rows, D_MODEL)),
            pl.BlockSpec((1, D_MODEL, D_MODEL), lambda l, j: (l, 0, j)),
            pl.BlockSpec((1, 1, D_MODEL), lambda l, j: (l, 0, j)),
        ],
        out_specs=pl.BlockSpec((1, rows, D_MODEL), lambda l, j: (l, 0, j)),
        out_shape=jax.ShapeDtypeStruct((depth, rows, N_MOD * D_MODEL), F32),
        compiler_params=pltpu.CompilerParams(
            dimension_semantics=("arbitrary", "arbitrary"), vmem_limit_bytes=VMEM_LIMIT),
        name="modulation",
    )(cvecs, w_mod, b_mod.reshape(depth, 1, N_MOD * D_MODEL))


def _in_proj_kernel(x_ref, shift_ref, scale_ref, gpre_ref, win_ref, gq_ref, wq_ref, gkv_ref,
                    wk_ref, wvt_ref, cos_ref, sin_ref, q_ref, k_ref, vt_ref, cv_ref):
    x = x_ref[0]
    gain = gpre_ref[...] * (1.0 + scale_ref[0])
    h = (x * _rms_scale(x, D_MODEL)) * gain + shift_ref[0]
    z = jnp.dot(h.astype(BF16), win_ref[...], preferred_element_type=F32)
    cos = cos_ref[...]
    sin = sin_ref[...]

    zq = z[:, C_Q:C_Q + Q_LORA]
    zqn = (zq * _rms_scale(zq, Q_LORA)) * gq_ref[...]
    qq = jnp.dot(zqn.astype(BF16), wq_ref[...], preferred_element_type=F32)
    half = MLA_HEADS * HEAD_PAD
    for hd in range(MLA_HEADS):
        lo = hd * HEAD_PAD
        qh = qq[:, lo:lo + HEAD_PAD] * cos + qq[:, half + lo:half + lo + HEAD_PAD] * sin
        q_ref[0, :, lo:lo + HEAD_PAD] = qh.astype(BF16)

    ckv = z[:, C_CKV:C_CKV + KV_LORA]
    ckvn = ((ckv * _rms_scale(ckv, KV_LORA)) * gkv_ref[...]).astype(BF16)
    kk = jnp.dot(ckvn, wk_ref[...], preferred_element_type=F32)
    kr = z[:, C_KR:C_KR + HEAD_PAD] * cos + z[:, C_KR_ROT:C_KR_ROT + HEAD_PAD] * sin
    for hd in range(MLA_HEADS):
        lo = hd * HEAD_PAD
        k_ref[0, :, lo:lo + HEAD_PAD] = (kk[:, lo:lo + HEAD_PAD] + kr).astype(BF16)

    vt = lax.dot_general(wvt_ref[...], ckvn, (((1,), (1,)), ((), ())),
                         preferred_element_type=F32)
    ones = jnp.ones((V_ROWS - V_DIM, vt.shape[1]), BF16)
    for hd in range(MLA_HEADS):
        vt_ref[0, hd * V_ROWS:hd * V_ROWS + V_DIM, :] = (
            vt[hd * V_DIM:(hd + 1) * V_DIM].astype(BF16))
        vt_ref[0, hd * V_ROWS + V_DIM:(hd + 1) * V_ROWS, :] = ones

    a = z[:, C_CONF_A:C_CONF_A + CONF_WIDTH]
    g = z[:, C_CONF_G:C_CONF_G + CONF_WIDTH]
    cv_ref[0, :, 0:CONF_WIDTH] = a * jax.nn.sigmoid(g)
    cv_ref[0, :, CONF_WIDTH:CONF_WIDTH + SC_WIDTH] = (
        z[:, C_SC_C:C_SC_C + SC_WIDTH] * z[:, C_SC_H:C_SC_H + SC_WIDTH])
    cv_ref[0, :, CONF_WIDTH + SC_WIDTH:] = z[:, C_SC_B:C_SC_B + SC_WIDTH]


def _in_proj(x, shift, scale, wts, cos_t, sin_t, tm):
    bsz, seq, _ = x.shape
    n_t = seq // tm
    tok = lambda b, i: (b, i, 0)
    vec = lambda b, i: (b, 0, 0)
    cw = CONF_WIDTH + 2 * SC_WIDTH
    return pl.pallas_call(
        _in_proj_kernel,
        grid=(bsz, n_t),
        in_specs=[
            pl.BlockSpec((1, tm, D_MODEL), tok),
            pl.BlockSpec((1, 1, D_MODEL), vec),
            pl.BlockSpec((1, 1, D_MODEL), vec),
            _const_spec((1, D_MODEL)),
            _const_spec((D_MODEL, IN_PACKED)),
            _const_spec((1, Q_LORA)),
            _const_spec((Q_LORA, 2 * MLA_HEADS * HEAD_PAD)),
            _const_spec((1, KV_LORA)),
            _const_spec((KV_LORA, MLA_HEADS * HEAD_PAD)),
            _const_spec((MLA_WIDTH, KV_LORA)),
            pl.BlockSpec((tm, HEAD_PAD), lambda b, i: (i, 0)),
            pl.BlockSpec((tm, HEAD_PAD), lambda b, i: (i, 0)),
        ],
        out_specs=[
            pl.BlockSpec((1, tm, MLA_HEADS * HEAD_PAD), tok),
            pl.BlockSpec((1, tm, MLA_HEADS * HEAD_PAD), tok),
            pl.BlockSpec((1, MLA_HEADS * V_ROWS, tm), lambda b, i: (b, 0, i)),
            pl.BlockSpec((1, tm, cw), tok),
        ],
        out_shape=[
            jax.ShapeDtypeStruct((bsz, seq, MLA_HEADS * HEAD_PAD), BF16),
            jax.ShapeDtypeStruct((bsz, seq, MLA_HEADS * HEAD_PAD), BF16),
            jax.ShapeDtypeStruct((bsz, MLA_HEADS * V_ROWS, seq), BF16),
            jax.ShapeDtypeStruct((bsz, seq, cw), F32),
        ],
        compiler_params=pltpu.CompilerParams(
            dimension_semantics=("arbitrary", "arbitrary"), vmem_limit_bytes=VMEM_LIMIT),
        name="in_proj",
    )(x, shift, scale, wts["g_pre_mix"], wts["w_in"], wts["g_q"], wts["w_q"], wts["g_kv"],
      wts["w_k"], wts["w_vt"], cos_t, sin_t)


def _attn_kernel(q_ref, k_ref, vt_ref, o_ref, sa_ref, sb_ref, acc_ref, *, tk):
    tq = q_ref.shape[1]
    qt = q_ref[0].astype(F32).T.astype(BF16)
    n_chunks = k_ref.shape[1] // tk

    def key_start(c):
        return c * tk if isinstance(c, int) else pl.multiple_of(c * tk, tk)

    def scores(c, s_ref):
        kc = k_ref[0, pl.ds(key_start(c), tk), :]
        s = jnp.dot(kc, qt, preferred_element_type=F32)
        s_ref[...] = s
        return jnp.max(s, axis=0, keepdims=True)

    def accumulate(c, s_ref, s_max, m):
        m_new = jnp.maximum(m, s_max)
        alpha = jnp.exp2(m - m_new)
        p = jnp.exp2(s_ref[...] - m_new).astype(BF16)
        vc = vt_ref[0, :, pl.ds(key_start(c), tk)]
        acc_ref[...] = acc_ref[...] * alpha + jnp.dot(vc, p, preferred_element_type=F32)
        return m_new

    acc_ref[...] = jnp.zeros_like(acc_ref)
    m = jnp.full((1, tq), NEG_BIG, F32)
    max_a = scores(0, sa_ref)
    n_pairs = (n_chunks - 1) // 2

    def body(j, carry):
        max_a, m = carry
        max_b = scores(2 * j + 1, sb_ref)
        m = accumulate(2 * j, sa_ref, max_a, m)
        max_a = scores(2 * j + 2, sa_ref)
        m = accumulate(2 * j + 1, sb_ref, max_b, m)
        return max_a, m

    if n_pairs > 0:
        max_a, m = lax.fori_loop(0, n_pairs, body, (max_a, m), unroll=2)
    if n_chunks % 2 == 0:
        max_b = scores(n_chunks - 1, sb_ref)
        m = accumulate(n_chunks - 2, sa_ref, max_a, m)
        accumulate(n_chunks - 1, sb_ref, max_b, m)
    else:
        accumulate(n_chunks - 1, sa_ref, max_a, m)

    acc = acc_ref[...]
    o_ref[0] = acc[0:V_DIM] * (1.0 / acc[V_DIM:V_DIM + 1])


def _attention(q, k, vt, tq, tk):
    bsz, seq, _ = q.shape
    t = k.shape[1]
    return pl.pallas_call(
        functools.partial(_attn_kernel, tk=tk),
        grid=(bsz, MLA_HEADS, seq // tq),
        in_specs=[
            pl.BlockSpec((1, tq, HEAD_PAD), lambda b, h, i: (b, i, h)),
            pl.BlockSpec((1, t, HEAD_PAD), lambda b, h, i: (b, 0, h)),
            pl.BlockSpec((1, V_ROWS, t), lambda b, h, i: (b, h, 0)),
        ],
        out_specs=pl.BlockSpec((1, V_DIM, tq), lambda b, h, i: (b, h, i)),
        out_shape=jax.ShapeDtypeStruct((bsz, MLA_WIDTH, seq), F32),
        scratch_shapes=[
            pltpu.VMEM((tk, tq), F32),
            pltpu.VMEM((tk, tq), F32),
            pltpu.VMEM((V_ROWS, tq), F32),
        ],
        compiler_params=pltpu.CompilerParams(
            dimension_semantics=("arbitrary", "arbitrary", "arbitrary"),
            vmem_limit_bytes=VMEM_LIMIT),
        name="attention",
    )(q, k, vt)


def _mix_out_kernel(x_ref, ot_ref, cv_ref, prev_ref, next_ref, gate_ref, gbr_ref, wo_ref,
                    cw_ref, cb_ref, lng_ref, lnb_ref, sw_ref, gpost_ref, o_ref,
                    uext_ref, cext_ref):
    tm = x_ref.shape[1]
    i = pl.program_id(1)
    has_prev = (i > 0).astype(F32)
    has_next = (i < pl.num_programs(1) - 1).astype(F32)
    c0, c1, c2 = 0, CONF_WIDTH, CONF_WIDTH + SC_WIDTH

    a = ot_ref[0].T
    an = (a * _rms_scale(a, MLA_WIDTH)) * gbr_ref[:, 0:MLA_WIDTH]
    y = jnp.dot(an.astype(BF16), wo_ref[0:MLA_WIDTH, :], preferred_element_type=F32)

    uext_ref[0:HALO, :] = prev_ref[0, :, c0:c1] * has_prev
    uext_ref[HALO:HALO + tm, :] = cv_ref[0, :, c0:c1]
    uext_ref[HALO + tm:, :] = next_ref[0, :, c0:c1] * has_next
    base = HALO - CONF_K // 2
    conv = cb_ref[...] + cw_ref[0:1, :] * uext_ref[base:base + tm, :]
    for t in range(1, CONF_K):
        conv = conv + cw_ref[t:t + 1, :] * uext_ref[base + t:base + t + tm, :]
    mu = jnp.sum(conv, axis=-1, keepdims=True) * (1.0 / CONF_WIDTH)
    d = conv - mu
    ln = (d * _rms_scale(d, CONF_WIDTH)) * lng_ref[...] + lnb_ref[...]
    cf = ln * jax.nn.sigmoid(ln)
    cfn = (cf * _rms_scale(cf, CONF_WIDTH)) * gbr_ref[:, MLA_WIDTH:MLA_WIDTH + CONF_WIDTH]
    y = y + jnp.dot(cfn.astype(BF16), wo_ref[MLA_WIDTH:MLA_WIDTH + CONF_WIDTH, :],
                    preferred_element_type=F32)

    cext_ref[0:SUBLANES, :] = prev_ref[0, HALO - SUBLANES:, c1:c2] * has_prev
    cext_ref[SUBLANES:SUBLANES + tm, :] = cv_ref[0, :, c1:c2]
    cext_ref[SUBLANES + tm:, :] = next_ref[0, 0:SUBLANES, c1:c2] * has_next
    base = SUBLANES - SC_K // 2
    sc = sw_ref[0:1, :] * cext_ref[base:base + tm, :]
    for t in range(1, SC_K):
        sc = sc + sw_ref[t:t + 1, :] * cext_ref[base + t:base + t + tm, :]
    sc = cv_ref[0, :, c2:] * sc
    scn = (sc * _rms_scale(sc, SC_WIDTH)) * gbr_ref[:, MLA_WIDTH + CONF_WIDTH:]
    y = y + jnp.dot(scn.astype(BF16), wo_ref[MLA_WIDTH + CONF_WIDTH:, :],
                    preferred_element_type=F32)

    yn = (y * _rms_scale(y, D_MODEL)) * gpost_ref[...]
    o_ref[0] = x_ref[0] + gate_ref[0] * yn


def _mix_out(x, ot, cv, gate, wts, tm):
    bsz, seq, _ = x.shape
    n_t = seq // tm
    hb = tm // HALO
    n_h = seq // HALO
    cw = cv.shape[-1]
    tok = lambda b, i: (b, i, 0)
    return pl.pallas_call(
        _mix_out_kernel,
        grid=(bsz, n_t),
        in_specs=[
            pl.BlockSpec((1, tm, D_MODEL), tok),
            pl.BlockSpec((1, MLA_WIDTH, tm), lambda b, i: (b, 0, i)),
            pl.BlockSpec((1, tm, cw), tok),
            pl.BlockSpec((1, HALO, cw), lambda b, i: (b, jnp.maximum(i * hb - 1, 0), 0)),
            pl.BlockSpec((1, HALO, cw), lambda b, i: (b, jnp.minimum((i + 1) * hb, n_h - 1), 0)),
            pl.BlockSpec((1, 1, D_MODEL), lambda b, i: (b, 0, 0)),
            _const_spec((1, D_MODEL)),
            _const_spec((D_MODEL, D_MODEL)),
            _const_spec((CONF_K, CONF_WIDTH)),
            _const_spec((1, CONF_WIDTH)),
            _const_spec((1, CONF_WIDTH)),
            _const_spec((1, CONF_WIDTH)),
            _const_spec((SC_K, SC_WIDTH)),
            _const_spec((1, D_MODEL)),
        ],
        out_specs=pl.BlockSpec((1, tm, D_MODEL), tok),
        out_shape=jax.ShapeDtypeStruct((bsz, seq, D_MODEL), F32),
        scratch_shapes=[
            pltpu.VMEM((tm + 2 * HALO, CONF_WIDTH), F32),
            pltpu.VMEM((tm + 2 * SUBLANES, SC_WIDTH), F32),
        ],
        compiler_params=pltpu.CompilerParams(
            dimension_semantics=("arbitrary", "arbitrary"), vmem_limit_bytes=VMEM_LIMIT),
        name="mix_out",
    )(x, ot, cv, cv, cv, gate, wts["g_branch"], wts["w_o"], wts["conf_dw_w"], wts["conf_dw_b"],
      wts["conf_ln_g"], wts["conf_ln_b"], wts["sc_dw_w"], wts["g_post_mix"])


def _mlp_kernel(x_ref, shift_ref, scale_ref, gate_ref, gpre_ref, w1_ref, w2_ref, gpost_ref,
                o_ref, *, ff_chunk):
    x = x_ref[0]
    gain = gpre_ref[...] * (1.0 + scale_ref[0])
    h = ((x * _rms_scale(x, D_MODEL)) * gain + shift_ref[0]).astype(BF16)
    y = None
    for j in range(D_FF // ff_chunk):
        lo = j * ff_chunk
        a = jnp.maximum(jnp.dot(h, w1_ref[:, lo:lo + ff_chunk], preferred_element_type=F32), 0.0)
        part = jnp.dot((a * a).astype(BF16), w2_ref[lo:lo + ff_chunk, :],
                       preferred_element_type=F32)
        y = part if y is None else y + part
    yn = (y * _rms_scale(y, D_MODEL)) * gpost_ref[...]
    o_ref[0] = x + gate_ref[0] * yn


def _mlp(x, shift, scale, gate, wts, tm, ff_chunk=1024):
    bsz, seq, _ = x.shape
    tok = lambda b, i: (b, i, 0)
    vec = lambda b, i: (b, 0, 0)
    return pl.pallas_call(
        functools.partial(_mlp_kernel, ff_chunk=ff_chunk),
        grid=(bsz, seq // tm),
        in_specs=[
            pl.BlockSpec((1, tm, D_MODEL), tok),
            pl.BlockSpec((1, 1, D_MODEL), vec),
            pl.BlockSpec((1, 1, D_MODEL), vec),
            pl.BlockSpec((1, 1, D_MODEL), vec),
            _const_spec((1, D_MODEL)),
            pl.BlockSpec((D_MODEL, D_FF), lambda b, i: (0, 0), pipeline_mode=pl.Buffered(1)),
            pl.BlockSpec((D_FF, D_MODEL), lambda b, i: (0, 0), pipeline_mode=pl.Buffered(1)),
            _const_spec((1, D_MODEL)),
        ],
        out_specs=pl.BlockSpec((1, tm, D_MODEL), tok),
        out_shape=jax.ShapeDtypeStruct((bsz, seq, D_MODEL), F32),
        compiler_params=pltpu.CompilerParams(
            dimension_semantics=("arbitrary", "arbitrary"), vmem_limit_bytes=VMEM_LIMIT),
        name="mlp",
    )(x, shift, scale, gate, wts["g_pre_mlp"], wts["w_mlp_in"], wts["w_mlp_out"],
      wts["g_post_mlp"])


def _rotate_half_cols(w):
    hh = AXIS_DIM // 2
    parts = []
    for ax in range(2):
        blk = w[..., ax * AXIS_DIM:(ax + 1) * AXIS_DIM]
        parts.append(jnp.concatenate([-blk[..., hh:], blk[..., :hh]], axis=-1))
    return jnp.concatenate(parts, axis=-1)


def _head_group(nope, rope):
    pad = jnp.zeros(nope.shape[:-1] + (HEAD_PAD - QK_DIM,), nope.dtype)
    return jnp.concatenate([nope, rope, pad], axis=-1)


def _pack_layer(i, w_in, g_q, w_q_b, g_kv, w_kv_b, p):
    win = w_in[i]
    zeros_nope = jnp.zeros((D_MODEL, QK_NOPE), F32)
    w_kr = win[:, Q_LORA + KV_LORA:Q_LORA + KV_LORA + QK_ROPE]
    conf0 = Q_LORA + KV_LORA + QK_ROPE
    sc0 = conf0 + 2 * CONF_WIDTH
    win_packed = jnp.concatenate([
        win[:, 0:Q_LORA],
        win[:, Q_LORA:Q_LORA + KV_LORA],
        _head_group(zeros_nope, w_kr),
        _head_group(zeros_nope, _rotate_half_cols(w_kr)),
        win[:, conf0:sc0],
        win[:, sc0:],
    ], axis=-1)

    wq = w_q_b[i].reshape(Q_LORA, MLA_HEADS, QK_DIM)
    wq_nope, wq_rope = wq[..., :QK_NOPE], wq[..., QK_NOPE:]
    wq_main = _head_group(wq_nope, wq_rope).reshape(Q_LORA, MLA_HEADS * HEAD_PAD)
    wq_rot = _head_group(jnp.zeros_like(wq_nope), _rotate_half_cols(wq_rope)).reshape(
        Q_LORA, MLA_HEADS * HEAD_PAD)

    wkv = w_kv_b[i].reshape(KV_LORA, MLA_HEADS, QK_NOPE + V_DIM)
    wk = _head_group(wkv[..., :QK_NOPE], jnp.zeros((KV_LORA, MLA_HEADS, QK_ROPE), F32)).reshape(
        KV_LORA, MLA_HEADS * HEAD_PAD)
    wv_t = jnp.transpose(wkv[..., QK_NOPE:], (1, 2, 0))

    row = lambda v: v[i][None, :]
    return {
        "g_pre_mix": row(p["g_pre_mix"]),
        "w_in": win_packed.astype(BF16),
        "g_q": row(g_q) * (SM_SCALE * math.log2(math.e)),
        "w_q": jnp.concatenate([wq_main, wq_rot], axis=-1).astype(BF16),
        "g_kv": row(g_kv),
        "w_k": wk.astype(BF16),
        "w_vt": wv_t.reshape(MLA_WIDTH, KV_LORA).astype(BF16),
        "g_branch": row(p["g_branch"]),
        "w_o": p["w_o"][i].astype(BF16),
        "conf_dw_w": p["conf_dw_w"][i],
        "conf_dw_b": row(p["conf_dw_b"]),
        "conf_ln_g": row(p["conf_ln_g"]),
        "conf_ln_b": row(p["conf_ln_b"]),
        "sc_dw_w": p["sc_dw_w"][i],
        "g_post_mix": row(p["g_post_mix"]),
        "g_pre_mlp": row(p["g_pre_mlp"]),
        "w_mlp_in": p["w_mlp_in"][i].astype(BF16),
        "w_mlp_out": p["w_mlp_out"][i].astype(BF16),
        "g_post_mlp": row(p["g_post_mlp"]),
    }


def _rope_tables(n_lat):
    rows = n_lat // GRID_W
    row = jnp.broadcast_to(jnp.arange(rows)[:, None], (rows, GRID_W)).reshape(-1).astype(F32)
    col = jnp.broadcast_to(jnp.arange(GRID_W)[None, :], (rows, GRID_W)).reshape(-1).astype(F32)
    inv = 1.0 / (ROPE_THETA ** (jnp.arange(0, AXIS_DIM, 2, dtype=F32) / AXIS_DIM))
    ar = row[:, None] * inv
    ac = col[:, None] * inv
    ang = jnp.concatenate([ar, ar, ac, ac], axis=-1)
    return (_head_group(jnp.ones((n_lat, QK_NOPE), F32), jnp.cos(ang)),
            _head_group(jnp.zeros((n_lat, QK_NOPE), F32), jnp.sin(ang)))


def _identity_tables(n):
    return (_head_group(jnp.ones((n, QK_NOPE), F32), jnp.ones((n, QK_ROPE), F32)),
            jnp.zeros((n, HEAD_PAD), F32))


def kernel(x, c, ctx, c_ctx, w_mod, b_mod, g_pre_mix, g_post_mix, g_pre_mlp, g_post_mlp, w_in,
           g_q, w_q_b, g_kv, w_kv_b, conf_dw_w, conf_dw_b, conf_ln_g, conf_ln_b, sc_dw_w,
           g_branch, w_o, w_mlp_in, w_mlp_out):
    bsz, n_lat, _ = x.shape
    n_ctx = ctx.shape[1]
    depth = w_mod.shape[0]
    params = dict(g_pre_mix=g_pre_mix, g_post_mix=g_post_mix, g_pre_mlp=g_pre_mlp,
                  g_post_mlp=g_post_mlp, conf_dw_w=conf_dw_w, conf_dw_b=conf_dw_b,
                  conf_ln_g=conf_ln_g, conf_ln_b=conf_ln_b, sc_dw_w=sc_dw_w, g_branch=g_branch,
                  w_o=w_o, w_mlp_in=w_mlp_in, w_mlp_out=w_mlp_out)

    mod_rows = SUBLANES * pl.cdiv(bsz + 1, SUBLANES)
    cvecs = jnp.concatenate(
        [c, c_ctx[None, :], jnp.zeros((mod_rows - bsz - 1, D_MODEL), F32)], axis=0)
    mod = _modulation(cvecs, w_mod, b_mod).reshape(depth, mod_rows, N_MOD, D_MODEL)

    rope_lat = _rope_tables(n_lat)
    rope_ctx = _identity_tables(n_ctx)
    tm_lat, tm_ctx = 256, min(256, n_ctx)
    tq_lat, tq_ctx = 512, min(256, n_ctx)
    tk_lat, tk_ctx = 768, min(256, n_ctx)

    xl, xc = x, ctx
    for i in range(depth):
        last = i == depth - 1
        wts = _pack_layer(i, w_in, g_q, w_q_b, g_kv, w_kv_b, params)
        ml = [mod[i, :bsz, j][:, None, :] for j in range(N_MOD)]
        mc = [jnp.broadcast_to(mod[i, bsz, j][None, None, :], (bsz, 1, D_MODEL))
              for j in range(N_MOD)]

        q_c, k_c, vt_c, cv_c = _in_proj(xc, mc[0], mc[1], wts, *rope_ctx, tm_ctx)
        q_l, k_l, vt_l, cv_l = _in_proj(xl, ml[0], ml[1], wts, *rope_lat, tm_lat)

        k_all = jnp.concatenate([k_c, k_l], axis=1)
        vt_all = jnp.concatenate([vt_c, vt_l], axis=2)
        ot_l = _attention(q_l, k_all, vt_all, tq_lat, tk_lat)
        xl = _mix_out(xl, ot_l, cv_l, ml[2], wts, tm_lat)
        xl = _mlp(xl, ml[3], ml[4], ml[5], wts, 512)

        if not last:
            ot_c = _attention(q_c, k_c, vt_c, tq_ctx, tk_ctx)
            xc = _mix_out(xc, ot_c, cv_c, mc[2], wts, tm_ctx)
            xc = _mlp(xc, mc[3], mc[4], mc[5], wts, min(256, n_ctx))
    return xl
```

```python
import functools
import math

import jax
import jax.numpy as jnp
from jax import lax
from jax.experimental import pallas as pl
from jax.experimental.pallas import tpu as pltpu

F32 = jnp.float32
BF16 = jnp.bfloat16

D_MODEL = 1024
GRID_W = 64
N_MOD = 6
MLA_HEADS = 8
QK_NOPE = 64
QK_ROPE = 32
QK_DIM = QK_NOPE + QK_ROPE
V_DIM = 64
MLA_WIDTH = MLA_HEADS * V_DIM
Q_LORA = 256
KV_LORA = 128
AXIS_DIM = QK_ROPE // 2
ROPE_THETA = 10000.0
CONF_WIDTH = 256
CONF_K = 31
SC_WIDTH = 256
SC_K = 3
D_FF = 4 * D_MODEL
EPS = 1e-6
SM_SCALE = 1.0 / math.sqrt(QK_DIM)

LANES = 128
SUBLANES = 8
BF16_ROWS = 16

HEAD_PAD = LANES
V_ROWS = V_DIM + BF16_ROWS
HALO = 16

C_Q = 0
C_CKV = C_Q + Q_LORA
C_KR = C_CKV + KV_LORA
C_KR_ROT = C_KR + HEAD_PAD
C_CONF_A = C_KR_ROT + HEAD_PAD
C_CONF_G = C_CONF_A + CONF_WIDTH
C_SC_B = C_CONF_G + CONF_WIDTH
C_SC_C = C_SC_B + SC_WIDTH
C_SC_H = C_SC_C + SC_WIDTH
IN_PACKED = C_SC_H + SC_WIDTH

NEG_BIG = -1e30
VMEM_LIMIT = 56 * 1024 * 1024


def _const_spec(shape):
    nd = len(shape)
    return pl.BlockSpec(shape, lambda *_: (0,) * nd)


def _rms_scale(v, width):
    return lax.rsqrt(jnp.sum(v * v, axis=-1, keepdims=True) * (1.0 / width) + EPS)


def _mod_kernel(c_ref, w_ref, b_ref, o_ref):
    cv = c_ref[...]
    s = cv * jax.nn.sigmoid(cv)
    o_ref[0] = jnp.dot(s, w_ref[0], precision=lax.Precision.HIGHEST,
                       preferred_element_type=F32) + b_ref[0]


def _modulation(cvecs, w_mod, b_mod):
    depth = w_mod.shape[0]
    rows = cvecs.shape[0]
    return pl.pallas_call(
        _mod_kernel,
        grid=(depth, N_MOD),
        in_specs=[
            _const_spec((rows, D_MODEL)),
            pl.BlockSpec((1, D_MODEL, D_MODEL), lambda l, j: (l, 0, j)),
            pl.BlockSpec((1, 1, D_MODEL), lambda l, j: (l, 0, j)),
        ],
        out_specs=pl.BlockSpec((1, rows, D_MODEL), lambda l, j: (l, 0, j)),
        out_shape=jax.ShapeDtypeStruct((depth, rows, N_MOD * D_MODEL), F32),
        compiler_params=pltpu.CompilerParams(
            dimension_semantics=("arbitrary", "arbitrary"), vmem_limit_bytes=VMEM_LIMIT),
        name="modulation",
    )(cvecs, w_mod, b_mod.reshape(depth, 1, N_MOD * D_MODEL))


def _in_proj_kernel(xl_ref, xc_ref, shl_ref, scl_ref, shc_ref, scc_ref, gpre_ref, win_ref, gq_ref,
                    wq_ref, gkv_ref, wk_ref, wvt_ref, cos_ref, sin_ref, q_ref, k_ref, vt_ref, cv_ref,
                    *, n_lat_tiles):
    is_ctx = pl.program_id(1) >= n_lat_tiles
    x = jnp.where(is_ctx, xc_ref[0], xl_ref[0])
    shift = jnp.where(is_ctx, shc_ref[0], shl_ref[0])
    scale = jnp.where(is_ctx, scc_ref[0], scl_ref[0])
    gain = gpre_ref[...] * (1.0 + scale)
    h = (x * _rms_scale(x, D_MODEL)) * gain + shift
    z = jnp.dot(h.astype(BF16), win_ref[...], preferred_element_type=F32)
    cos = cos_ref[...]
    sin = sin_ref[...]

    zq = z[:, C_Q:C_Q + Q_LORA]
    zqn = (zq * _rms_scale(zq, Q_LORA)) * gq_ref[...]
    qq = jnp.dot(zqn.astype(BF16), wq_ref[...], preferred_element_type=F32)
    half = MLA_HEADS * HEAD_PAD
    for hd in range(MLA_HEADS):
        lo = hd * HEAD_PAD
        qh = qq[:, lo:lo + HEAD_PAD] * cos + qq[:, half + lo:half + lo + HEAD_PAD] * sin
        q_ref[0, :, lo:lo + HEAD_PAD] = qh.astype(BF16)

    ckv = z[:, C_CKV:C_CKV + KV_LORA]
    ckvn = ((ckv * _rms_scale(ckv, KV_LORA)) * gkv_ref[...]).astype(BF16)
    kk = jnp.dot(ckvn, wk_ref[...], preferred_element_type=F32)
    kr = z[:, C_KR:C_KR + HEAD_PAD] * cos + z[:, C_KR_ROT:C_KR_ROT + HEAD_PAD] * sin
    for hd in range(MLA_HEADS):
        lo = hd * HEAD_PAD
        k_ref[0, :, lo:lo + HEAD_PAD] = (kk[:, lo:lo + HEAD_PAD] + kr).astype(BF16)

    vt = lax.dot_general(wvt_ref[...], ckvn, (((1,), (1,)), ((), ())),
                         preferred_element_type=F32)
    ones = jnp.ones((V_ROWS - V_DIM, vt.shape[1]), BF16)
    for hd in range(MLA_HEADS):
        vt_ref[0, hd * V_ROWS:hd * V_ROWS + V_DIM, :] = (
            vt[hd * V_DIM:(hd + 1) * V_DIM].astype(BF16))
        vt_ref[0, hd * V_ROWS + V_DIM:(hd + 1) * V_ROWS, :] = ones

    a = z[:, C_CONF_A:C_CONF_A + CONF_WIDTH]
    g = z[:, C_CONF_G:C_CONF_G + CONF_WIDTH]
    cv_ref[0, :, 0:CONF_WIDTH] = a * jax.nn.sigmoid(g)
    cv_ref[0, :, CONF_WIDTH:CONF_WIDTH + SC_WIDTH] = (
        z[:, C_SC_C:C_SC_C + SC_WIDTH] * z[:, C_SC_H:C_SC_H + SC_WIDTH])
    cv_ref[0, :, CONF_WIDTH + SC_WIDTH:] = z[:, C_SC_B:C_SC_B + SC_WIDTH]


def _in_proj(xl, xc, mod_l, mod_c, wts, cos_t, sin_t, tm):
    bsz, n_lat, _ = xl.shape
    n_ctx = xc.shape[1]
    n_lt, n_ct = n_lat // tm, n_ctx // tm
    n_all = n_lat + n_ctx
    tok = lambda b, i: (b, i, 0)
    vec = lambda b, i: (b, 0, 0)
    cw = CONF_WIDTH + 2 * SC_WIDTH
    return pl.pallas_call(
        functools.partial(_in_proj_kernel, n_lat_tiles=n_lt),
        grid=(bsz, n_lt + n_ct),
        in_specs=[
            pl.BlockSpec((1, tm, D_MODEL), lambda b, i: (b, jnp.minimum(i, n_lt - 1), 0)),
            pl.BlockSpec((1, tm, D_MODEL), lambda b, i: (b, jnp.maximum(i - n_lt, 0), 0)),
            pl.BlockSpec((1, 1, D_MODEL), vec),
            pl.BlockSpec((1, 1, D_MODEL), vec),
            _const_spec((1, 1, D_MODEL)),
            _const_spec((1, 1, D_MODEL)),
            _const_spec((1, D_MODEL)),
            _const_spec((D_MODEL, IN_PACKED)),
            _const_spec((1, Q_LORA)),
            _const_spec((Q_LORA, 2 * MLA_HEADS * HEAD_PAD)),
            _const_spec((1, KV_LORA)),
            _const_spec((KV_LORA, MLA_HEADS * HEAD_PAD)),
            _const_spec((MLA_WIDTH, KV_LORA)),
            pl.BlockSpec((tm, HEAD_PAD), lambda b, i: (i, 0)),
            pl.BlockSpec((tm, HEAD_PAD), lambda b, i: (i, 0)),
        ],
        out_specs=[
            pl.BlockSpec((1, tm, MLA_HEADS * HEAD_PAD), tok),
            pl.BlockSpec((1, tm, MLA_HEADS * HEAD_PAD), tok),
            pl.BlockSpec((1, MLA_HEADS * V_ROWS, tm), lambda b, i: (b, 0, i)),
            pl.BlockSpec((1, tm, cw), tok),
        ],
        out_shape=[
            jax.ShapeDtypeStruct((bsz, n_all, MLA_HEADS * HEAD_PAD), BF16),
            jax.ShapeDtypeStruct((bsz, n_all, MLA_HEADS * HEAD_PAD), BF16),
            jax.ShapeDtypeStruct((bsz, MLA_HEADS * V_ROWS, n_all), BF16),
            jax.ShapeDtypeStruct((bsz, n_all, cw), F32),
        ],
        compiler_params=pltpu.CompilerParams(
            dimension_semantics=("arbitrary", "arbitrary"), vmem_limit_bytes=VMEM_LIMIT),
        name="in_proj",
    )(xl, xc, mod_l[0], mod_l[1], mod_c[0], mod_c[1], wts["g_pre_mix"], wts["w_in"], wts["g_q"],
      wts["w_q"], wts["g_kv"], wts["w_k"], wts["w_vt"], cos_t, sin_t)


def _attn_kernel(q_ref, k_ref, vt_ref, o_ref, qt_ref, sa_ref, sb_ref, acc_ref, *, tq, tk):
    n_q = q_ref.shape[1] // tq
    n_c = k_ref.shape[1] // tk
    total = n_q * n_c

    for i in range(n_q):
        qt_ref[:, i * tq:(i + 1) * tq] = (
            q_ref[0, i * tq:(i + 1) * tq, :].astype(F32).T.astype(BF16))
    acc_ref[...] = jnp.zeros_like(acc_ref)

    def advance(pos):
        qb, ck = pos
        wrap = ck == n_c - 1
        return jnp.where(wrap, qb + 1, qb), jnp.where(wrap, 0, ck + 1)

    def scores(pos, s_ref):
        qb, ck = pos
        qt = qt_ref[:, pl.ds(pl.multiple_of(qb * tq, tq), tq)]
        kc = k_ref[0, pl.ds(pl.multiple_of(ck * tk, tk), tk), :]
        s = jnp.dot(kc, qt, preferred_element_type=F32)
        s_ref[...] = s
        return jnp.max(s, axis=0, keepdims=True)

    def accumulate(pos, s_ref, s_max, m):
        qb, ck = pos
        m_prev = jnp.where(ck == 0, NEG_BIG, m)
        m_new = jnp.maximum(m_prev, s_max)
        alpha = jnp.exp2(m_prev - m_new)
        p = jnp.exp2(s_ref[...] - m_new).astype(BF16)
        vc = vt_ref[0, :, pl.ds(pl.multiple_of(ck * tk, tk), tk)]
        acc = acc_ref[...] * alpha + jnp.dot(vc, p, preferred_element_type=F32)
        acc_ref[...] = acc
        o_ref[0, :, pl.ds(pl.multiple_of(qb * tq, tq), tq)] = (
            acc[0:V_DIM] * (1.0 / acc[V_DIM:V_DIM + 1]))
        return m_new

    zero = jnp.int32(0)
    pos = (zero, zero)
    m = jnp.full((1, tq), NEG_BIG, F32)
    max_a = scores(pos, sa_ref)

    def body(_, carry):
        q0, c0, max_a, m = carry
        pos0 = (q0, c0)
        pos1 = advance(pos0)
        pos2 = advance(pos1)
        max_b = scores(pos1, sb_ref)
        m = accumulate(pos0, sa_ref, max_a, m)
        max_a = scores(pos2, sa_ref)
        m = accumulate(pos1, sb_ref, max_b, m)
        return pos2[0], pos2[1], max_a, m

    n_pairs = (total - 1) // 2
    if n_pairs > 0:
        q0, c0, max_a, m = lax.fori_loop(0, n_pairs, body, (zero, zero, max_a, m), unroll=2)
        pos = (q0, c0)
    if total % 2 == 0:
        pos1 = advance(pos)
        max_b = scores(pos1, sb_ref)
        m = accumulate(pos, sa_ref, max_a, m)
        accumulate(pos1, sb_ref, max_b, m)
    else:
        accumulate(pos, sa_ref, max_a, m)


def _attention(q, k, vt, seq, t, blk, tq, tk):
    bsz = q.shape[0]
    return pl.pallas_call(
        functools.partial(_attn_kernel, tq=tq, tk=tk),
        grid=(bsz, MLA_HEADS),
        in_specs=[
            pl.BlockSpec((1, seq, HEAD_PAD), lambda b, h: (b, blk, h)),
            pl.BlockSpec((1, t, HEAD_PAD), lambda b, h: (b, blk, h)),
            pl.BlockSpec((1, V_ROWS, t), lambda b, h: (b, h, blk)),
        ],
        out_specs=pl.BlockSpec((1, V_DIM, seq), lambda b, h: (b, h, 0)),
        out_shape=jax.ShapeDtypeStruct((bsz, MLA_WIDTH, seq), F32),
        scratch_shapes=[
            pltpu.VMEM((HEAD_PAD, seq), BF16),
            pltpu.VMEM((tk, tq), F32),
            pltpu.VMEM((tk, tq), F32),
            pltpu.VMEM((V_ROWS, tq), F32),
        ],
        compiler_params=pltpu.CompilerParams(
            dimension_semantics=("arbitrary", "arbitrary"), vmem_limit_bytes=VMEM_LIMIT),
        name="attention",
    )(q, k, vt)


def _mix_out_kernel(x_ref, ot_ref, cv_ref, prev_ref, next_ref, gate_ref, gbr_ref, wo_ref,
                    cw_ref, cb_ref, lng_ref, lnb_ref, sw_ref, gpost_ref, o_ref,
                    uext_ref, ush_ref, cext_ref):
    tm = x_ref.shape[1]
    i = pl.program_id(1)
    has_prev = (i > 0).astype(F32)
    has_next = (i < pl.num_programs(1) - 1).astype(F32)
    c0, c1, c2 = 0, CONF_WIDTH, CONF_WIDTH + SC_WIDTH

    a = ot_ref[0].T
    an = (a * _rms_scale(a, MLA_WIDTH)) * gbr_ref[:, 0:MLA_WIDTH]
    y = jnp.dot(an.astype(BF16), wo_ref[0:MLA_WIDTH, :], preferred_element_type=F32)

    uext_ref[0:HALO, :] = prev_ref[0, :, c0:c1] * has_prev
    uext_ref[HALO:HALO + tm, :] = cv_ref[0, :, c0:c1]
    uext_ref[HALO + tm:, :] = next_ref[0, :, c0:c1] * has_next
    base = HALO - CONF_K // 2
    n_a = ush_ref.shape[1] // SUBLANES - tm // SUBLANES + 1
    conv = cb_ref[...]
    for r in range(SUBLANES):
        ush_ref[r] = uext_ref[base + r:base + r + ush_ref.shape[1], :]
        for a_i in range(n_a):
            t = SUBLANES * a_i + r
            if t < CONF_K:
                conv = conv + cw_ref[t:t + 1, :] * ush_ref[r, SUBLANES * a_i:SUBLANES * a_i + tm, :]
    mu = jnp.sum(conv, axis=-1, keepdims=True) * (1.0 / CONF_WIDTH)
    d = conv - mu
    ln = (d * _rms_scale(d, CONF_WIDTH)) * lng_ref[...] + lnb_ref[...]
    cf = ln * jax.nn.sigmoid(ln)
    cfn = (cf * _rms_scale(cf, CONF_WIDTH)) * gbr_ref[:, MLA_WIDTH:MLA_WIDTH + CONF_WIDTH]
    y = y + jnp.dot(cfn.astype(BF16), wo_ref[MLA_WIDTH:MLA_WIDTH + CONF_WIDTH, :],
                    preferred_element_type=F32)

    cext_ref[0:SUBLANES, :] = prev_ref[0, HALO - SUBLANES:, c1:c2] * has_prev
    cext_ref[SUBLANES:SUBLANES + tm, :] = cv_ref[0, :, c1:c2]
    cext_ref[SUBLANES + tm:, :] = next_ref[0, 0:SUBLANES, c1:c2] * has_next
    base = SUBLANES - SC_K // 2
    sc = sw_ref[0:1, :] * cext_ref[base:base + tm, :]
    for t in range(1, SC_K):
        sc = sc + sw_ref[t:t + 1, :] * cext_ref[base + t:base + t + tm, :]
    sc = cv_ref[0, :, c2:] * sc
    scn = (sc * _rms_scale(sc, SC_WIDTH)) * gbr_ref[:, MLA_WIDTH + CONF_WIDTH:]
    y = y + jnp.dot(scn.astype(BF16), wo_ref[MLA_WIDTH + CONF_WIDTH:, :],
                    preferred_element_type=F32)

    yn = (y * _rms_scale(y, D_MODEL)) * gpost_ref[...]
    o_ref[0] = x_ref[0] + gate_ref[0] * yn


def _mix_out(x, ot, cv, row0, gate, wts, tm):
    bsz, seq, _ = x.shape
    n_t = seq // tm
    hb = tm // HALO
    t0, h0 = row0 // tm, row0 // HALO
    n_h = cv.shape[1] // HALO
    cw = cv.shape[-1]
    tok = lambda b, i: (b, i, 0)
    return pl.pallas_call(
        _mix_out_kernel,
        grid=(bsz, n_t),
        in_specs=[
            pl.BlockSpec((1, tm, D_MODEL), tok),
            pl.BlockSpec((1, MLA_WIDTH, tm), lambda b, i: (b, 0, i)),
            pl.BlockSpec((1, tm, cw), lambda b, i: (b, i + t0, 0)),
            pl.BlockSpec((1, HALO, cw), lambda b, i: (b, jnp.maximum(h0 + i * hb - 1, 0), 0)),
            pl.BlockSpec((1, HALO, cw),
                         lambda b, i: (b, jnp.minimum(h0 + (i + 1) * hb, n_h - 1), 0)),
            pl.BlockSpec((1, 1, D_MODEL), lambda b, i: (b, 0, 0)),
            _const_spec((1, D_MODEL)),
            _const_spec((D_MODEL, D_MODEL)),
            _const_spec((CONF_K, CONF_WIDTH)),
            _const_spec((1, CONF_WIDTH)),
            _const_spec((1, CONF_WIDTH)),
            _const_spec((1, CONF_WIDTH)),
            _const_spec((SC_K, SC_WIDTH)),
            _const_spec((1, D_MODEL)),
        ],
        out_specs=pl.BlockSpec((1, tm, D_MODEL), tok),
        out_shape=jax.ShapeDtypeStruct((bsz, seq, D_MODEL), F32),
        scratch_shapes=[
            pltpu.VMEM((tm + 2 * HALO, CONF_WIDTH), F32),
            pltpu.VMEM((SUBLANES, tm + SUBLANES * (pl.cdiv(CONF_K, SUBLANES) - 1), CONF_WIDTH), F32),
            pltpu.VMEM((tm + 2 * SUBLANES, SC_WIDTH), F32),
        ],
        compiler_params=pltpu.CompilerParams(
            dimension_semantics=("arbitrary", "arbitrary"), vmem_limit_bytes=VMEM_LIMIT),
        name="mix_out",
    )(x, ot, cv, cv, cv, gate, wts["g_branch"], wts["w_o"], wts["conf_dw_w"], wts["conf_dw_b"],
      wts["conf_ln_g"], wts["conf_ln_b"], wts["sc_dw_w"], wts["g_post_mix"])


def _mlp_kernel(x_ref, shift_ref, scale_ref, gate_ref, gpre_ref, w1_ref, w2_ref, gpost_ref,
                o_ref, *, ff_chunk):
    x = x_ref[0]
    gain = gpre_ref[...] * (1.0 + scale_ref[0])
    h = ((x * _rms_scale(x, D_MODEL)) * gain + shift_ref[0]).astype(BF16)
    y = None
    for j in range(D_FF // ff_chunk):
        lo = j * ff_chunk
        a = jnp.maximum(jnp.dot(h, w1_ref[:, lo:lo + ff_chunk], preferred_element_type=F32), 0.0)
        part = jnp.dot((a * a).astype(BF16), w2_ref[lo:lo + ff_chunk, :],
                       preferred_element_type=F32)
        y = part if y is None else y + part
    yn = (y * _rms_scale(y, D_MODEL)) * gpost_ref[...]
    o_ref[0] = x + gate_ref[0] * yn


def _mlp(x, shift, scale, gate, wts, tm, ff_chunk=1024):
    bsz, seq, _ = x.shape
    tok = lambda b, i: (b, i, 0)
    vec = lambda b, i: (b, 0, 0)
    return pl.pallas_call(
        functools.partial(_mlp_kernel, ff_chunk=ff_chunk),
        grid=(bsz, seq // tm),
        in_specs=[
            pl.BlockSpec((1, tm, D_MODEL), tok),
            pl.BlockSpec((1, 1, D_MODEL), vec),
            pl.BlockSpec((1, 1, D_MODEL), vec),
            pl.BlockSpec((1, 1, D_MODEL), vec),
            _const_spec((1, D_MODEL)),
            pl.BlockSpec((D_MODEL, D_FF), lambda b, i: (0, 0), pipeline_mode=pl.Buffered(1)),
            pl.BlockSpec((D_FF, D_MODEL), lambda b, i: (0, 0), pipeline_mode=pl.Buffered(1)),
            _const_spec((1, D_MODEL)),
        ],
        out_specs=pl.BlockSpec((1, tm, D_MODEL), tok),
        out_shape=jax.ShapeDtypeStruct((bsz, seq, D_MODEL), F32),
        compiler_params=pltpu.CompilerParams(
            dimension_semantics=("arbitrary", "arbitrary"), vmem_limit_bytes=VMEM_LIMIT),
        name="mlp",
    )(x, shift, scale, gate, wts["g_pre_mlp"], wts["w_mlp_in"], wts["w_mlp_out"],
      wts["g_post_mlp"])


def _rotate_half_cols(w):
    hh = AXIS_DIM // 2
    parts = []
    for ax in range(2):
        blk = w[..., ax * AXIS_DIM:(ax + 1) * AXIS_DIM]
        parts.append(jnp.concatenate([-blk[..., hh:], blk[..., :hh]], axis=-1))
    return jnp.concatenate(parts, axis=-1)


def _head_group(nope, rope):
    pad = jnp.zeros(nope.shape[:-1] + (HEAD_PAD - QK_DIM,), nope.dtype)
    return jnp.concatenate([nope, rope, pad], axis=-1)


def _pack_layer(i, w_in, g_q, w_q_b, g_kv, w_kv_b, p):
    win = w_in[i]
    zeros_nope = jnp.zeros((D_MODEL, QK_NOPE), F32)
    w_kr = win[:, Q_LORA + KV_LORA:Q_LORA + KV_LORA + QK_ROPE]
    conf0 = Q_LORA + KV_LORA + QK_ROPE
    sc0 = conf0 + 2 * CONF_WIDTH
    win_packed = jnp.concatenate([
        win[:, 0:Q_LORA],
        win[:, Q_LORA:Q_LORA + KV_LORA],
        _head_group(zeros_nope, w_kr),
        _head_group(zeros_nope, _rotate_half_cols(w_kr)),
        win[:, conf0:sc0],
        win[:, sc0:],
    ], axis=-1)

    wq = w_q_b[i].reshape(Q_LORA, MLA_HEADS, QK_DIM)
    wq_nope, wq_rope = wq[..., :QK_NOPE], wq[..., QK_NOPE:]
    wq_main = _head_group(wq_nope, wq_rope).reshape(Q_LORA, MLA_HEADS * HEAD_PAD)
    wq_rot = _head_group(jnp.zeros_like(wq_nope), _rotate_half_cols(wq_rope)).reshape(
        Q_LORA, MLA_HEADS * HEAD_PAD)

    wkv = w_kv_b[i].reshape(KV_LORA, MLA_HEADS, QK_NOPE + V_DIM)
    wk = _head_group(wkv[..., :QK_NOPE], jnp.zeros((KV_LORA, MLA_HEADS, QK_ROPE), F32)).reshape(
        KV_LORA, MLA_HEADS * HEAD_PAD)
    wv_t = jnp.transpose(wkv[..., QK_NOPE:], (1, 2, 0))

    row = lambda v: v[i][None, :]
    return {
        "g_pre_mix": row(p["g_pre_mix"]),
        "w_in": win_packed.astype(BF16),
        "g_q": row(g_q) * (SM_SCALE * math.log2(math.e)),
        "w_q": jnp.concatenate([wq_main, wq_rot], axis=-1).astype(BF16),
        "g_kv": row(g_kv),
        "w_k": wk.astype(BF16),
        "w_vt": wv_t.reshape(MLA_WIDTH, KV_LORA).astype(BF16),
        "g_branch": row(p["g_branch"]),
        "w_o": p["w_o"][i].astype(BF16),
        "conf_dw_w": p["conf_dw_w"][i],
        "conf_dw_b": row(p["conf_dw_b"]),
        "conf_ln_g": row(p["conf_ln_g"]),
        "conf_ln_b": row(p["conf_ln_b"]),
        "sc_dw_w": p["sc_dw_w"][i],
        "g_post_mix": row(p["g_post_mix"]),
        "g_pre_mlp": row(p["g_pre_mlp"]),
        "w_mlp_in": p["w_mlp_in"][i].astype(BF16),
        "w_mlp_out": p["w_mlp_out"][i].astype(BF16),
        "g_post_mlp": row(p["g_post_mlp"]),
    }


def _rope_tables(n_lat):
    rows = n_lat // GRID_W
    row = jnp.broadcast_to(jnp.arange(rows)[:, None], (rows, GRID_W)).reshape(-1).astype(F32)
    col = jnp.broadcast_to(jnp.arange(GRID_W)[None, :], (rows, GRID_W)).reshape(-1).astype(F32)
    inv = 1.0 / (ROPE_THETA ** (jnp.arange(0, AXIS_DIM, 2, dtype=F32) / AXIS_DIM))
    ar = row[:, None] * inv
    ac = col[:, None] * inv
    ang = jnp.concatenate([ar, ar, ac, ac], axis=-1)
    return (_head_group(jnp.ones((n_lat, QK_NOPE), F32), jnp.cos(ang)),
            _head_group(jnp.zeros((n_lat, QK_NOPE), F32), jnp.sin(ang)))


def _identity_tables(n):
    return (_head_group(jnp.ones((n, QK_NOPE), F32), jnp.ones((n, QK_ROPE), F32)),
            jnp.zeros((n, HEAD_PAD), F32))


def kernel(x, c, ctx, c_ctx, w_mod, b_mod, g_pre_mix, g_post_mix, g_pre_mlp, g_post_mlp, w_in,
           g_q, w_q_b, g_kv, w_kv_b, conf_dw_w, conf_dw_b, conf_ln_g, conf_ln_b, sc_dw_w,
           g_branch, w_o, w_mlp_in, w_mlp_out):
    bsz, n_lat, _ = x.shape
    n_ctx = ctx.shape[1]
    depth = w_mod.shape[0]
    params = dict(g_pre_mix=g_pre_mix, g_post_mix=g_post_mix, g_pre_mlp=g_pre_mlp,
                  g_post_mlp=g_post_mlp, conf_dw_w=conf_dw_w, conf_dw_b=conf_dw_b,
                  conf_ln_g=conf_ln_g, conf_ln_b=conf_ln_b, sc_dw_w=sc_dw_w, g_branch=g_branch,
                  w_o=w_o, w_mlp_in=w_mlp_in, w_mlp_out=w_mlp_out)

    mod_rows = SUBLANES * pl.cdiv(bsz + 1, SUBLANES)
    cvecs = jnp.concatenate(
        [c, c_ctx[None, :], jnp.zeros((mod_rows - bsz - 1, D_MODEL), F32)], axis=0)
    mod = _modulation(cvecs, w_mod, b_mod).reshape(depth, mod_rows, N_MOD, D_MODEL)

    cos_l, sin_l = _rope_tables(n_lat)
    cos_c, sin_c = _identity_tables(n_ctx)
    cos_t = jnp.concatenate([cos_l, cos_c], axis=0)
    sin_t = jnp.concatenate([sin_l, sin_c], axis=0)
    n_all = n_lat + n_ctx
    tm = min(256, n_ctx)
    tq_lat, tk_lat = 512, 768
    assert n_lat % tm == 0 and n_ctx % tm == 0 and n_lat % n_ctx == 0
    assert n_lat % tq_lat == 0 and n_all % tk_lat == 0
    ctx_blk = n_lat // n_ctx

    xl, xc = x, ctx
    for i in range(depth):
        last = i == depth - 1
        wts = _pack_layer(i, w_in, g_q, w_q_b, g_kv, w_kv_b, params)
        ml = [mod[i, :bsz, j][:, None, :] for j in range(N_MOD)]
        mc1 = [mod[i, bsz, j][None, None, :] for j in range(N_MOD)]

        q, k, vt, cv = _in_proj(xl, xc, ml[0:2], mc1[0:2], wts, cos_t, sin_t, tm)

        ot_l = _attention(q, k, vt, n_lat, n_all, 0, tq_lat, tk_lat)
        xl = _mix_out(xl, ot_l, cv, 0, ml[2], wts, tm)
        xl = _mlp(xl, ml[3], ml[4], ml[5], wts, 512)

        if not last:
            mc = [jnp.broadcast_to(v, (bsz, 1, D_MODEL)) for v in mc1]
            ot_c = _attention(q, k, vt, n_ctx, n_ctx, ctx_blk, n_ctx, n_ctx)
            xc = _mix_out(xc, ot_c, cv, n_lat, mc[2], wts, tm)
            xc = _mlp(xc, mc[3], mc[4], mc[5], wts, tm)
    return xl
```

```python
import functools
import math

import jax
import jax.numpy as jnp
from jax import lax
from jax.experimental import pallas as pl
from jax.experimental.pallas import tpu as pltpu

F32 = jnp.float32
BF16 = jnp.bfloat16

D_MODEL = 1024
GRID_W = 64
N_MOD = 6
MLA_HEADS = 8
QK_NOPE = 64
QK_ROPE = 32
QK_DIM = QK_NOPE + QK_ROPE
V_DIM = 64
MLA_WIDTH = MLA_HEADS * V_DIM
Q_LORA = 256
KV_LORA = 128
AXIS_DIM = QK_ROPE // 2
ROPE_THETA = 10000.0
CONF_WIDTH = 256
CONF_K = 31
SC_WIDTH = 256
SC_K = 3
D_FF = 4 * D_MODEL
EPS = 1e-6
SM_SCALE = 1.0 / math.sqrt(QK_DIM)

LANES = 128
SUBLANES = 8
BF16_ROWS = 16
MXU_DIM = 256

HEAD_PAD = LANES
V_ROWS = V_DIM + BF16_ROWS
HALO = 16

C_Q = 0
C_CKV = C_Q + Q_LORA
C_KR = C_CKV + KV_LORA
C_KR_ROT = C_KR + HEAD_PAD
C_CONF_A = C_KR_ROT + HEAD_PAD
C_CONF_G = C_CONF_A + CONF_WIDTH
C_SC_B = C_CONF_G + CONF_WIDTH
C_SC_C = C_SC_B + SC_WIDTH
C_SC_H = C_SC_C + SC_WIDTH
IN_PACKED = C_SC_H + SC_WIDTH

NEG_BIG = -1e30
PAIR_UNROLL = 4
VMEM_LIMIT = 56 * 1024 * 1024


def _const_spec(shape):
    nd = len(shape)
    return pl.BlockSpec(shape, lambda *_: (0,) * nd)


def _rms_scale(v, width):
    return lax.rsqrt(jnp.sum(v * v, axis=-1, keepdims=True) * (1.0 / width) + EPS)


def _mod_kernel(c_ref, w_ref, b_ref, o_ref):
    cv = c_ref[...]
    s = cv * jax.nn.sigmoid(cv)
    w = w_ref[0]
    s_hi = s.astype(BF16)
    s_lo = (s - s_hi.astype(F32)).astype(BF16)
    w_hi = w.astype(BF16)
    w_lo = (w - w_hi.astype(F32)).astype(BF16)
    acc = jnp.dot(s_hi, w_lo, preferred_element_type=F32)
    acc = acc + jnp.dot(s_lo, w_hi, preferred_element_type=F32)
    acc = acc + jnp.dot(s_hi, w_hi, preferred_element_type=F32)
    o_ref[0] = acc + b_ref[0]


def _modulation(cvecs, w_mod, b_mod):
    depth = w_mod.shape[0]
    rows = cvecs.shape[0]
    return pl.pallas_call(
        _mod_kernel,
        grid=(depth, N_MOD),
        in_specs=[
            _const_spec((rows, D_MODEL)),
            pl.BlockSpec((1, D_MODEL, D_MODEL), lambda l, j: (l, 0, j)),
            pl.BlockSpec((1, 1, D_MODEL), lambda l, j: (l, 0, j)),
        ],
        out_specs=pl.BlockSpec((1, rows, D_MODEL), lambda l, j: (l, 0, j)),
        out_shape=jax.ShapeDtypeStruct((depth, rows, N_MOD * D_MODEL), F32),
        compiler_params=pltpu.CompilerParams(
            dimension_semantics=("arbitrary", "arbitrary"), vmem_limit_bytes=VMEM_LIMIT),
        name="modulation",
    )(cvecs, w_mod, b_mod.reshape(depth, 1, N_MOD * D_MODEL))


def _in_proj_kernel(xl_ref, xc_ref, shl_ref, scl_ref, shc_ref, scc_ref, gpre_ref, win_ref, gq_ref,
                    wq_ref, gkv_ref, wk_ref, wvt_ref, cos_ref, sin_ref, q_ref, k_ref, vt_ref, cv_ref,
                    *, n_lat_tiles):
    is_ctx = pl.program_id(1) >= n_lat_tiles
    x = jnp.where(is_ctx, xc_ref[0], xl_ref[0])
    shift = jnp.where(is_ctx, shc_ref[0], shl_ref[0])
    scale = jnp.where(is_ctx, scc_ref[0], scl_ref[0])
    gain = gpre_ref[...] * (1.0 + scale)
    h = (x * _rms_scale(x, D_MODEL)) * gain + shift
    z = jnp.dot(h.astype(BF16), win_ref[...], preferred_element_type=F32)
    cos = cos_ref[...]
    sin = sin_ref[...]

    zq = z[:, C_Q:C_Q + Q_LORA]
    zqn = (zq * _rms_scale(zq, Q_LORA)) * gq_ref[...]
    qq = jnp.dot(zqn.astype(BF16), wq_ref[...], preferred_element_type=F32)
    half = MLA_HEADS * HEAD_PAD
    for hd in range(MLA_HEADS):
        lo = hd * HEAD_PAD
        qh = qq[:, lo:lo + HEAD_PAD] * cos + qq[:, half + lo:half + lo + HEAD_PAD] * sin
        q_ref[0, :, lo:lo + HEAD_PAD] = qh.astype(BF16)

    ckv = z[:, C_CKV:C_CKV + KV_LORA]
    ckvn = ((ckv * _rms_scale(ckv, KV_LORA)) * gkv_ref[...]).astype(BF16)
    kk = jnp.dot(ckvn, wk_ref[...], preferred_element_type=F32)
    kr = z[:, C_KR:C_KR + HEAD_PAD] * cos + z[:, C_KR_ROT:C_KR_ROT + HEAD_PAD] * sin
    for hd in range(MLA_HEADS):
        lo = hd * HEAD_PAD
        k_ref[0, :, lo:lo + HEAD_PAD] = (kk[:, lo:lo + HEAD_PAD] + kr).astype(BF16)

    vt = lax.dot_general(wvt_ref[...], ckvn, (((1,), (1,)), ((), ())),
                         preferred_element_type=F32)
    ones = jnp.ones((V_ROWS - V_DIM, vt.shape[1]), BF16)
    for hd in range(MLA_HEADS):
        vt_ref[0, hd * V_ROWS:hd * V_ROWS + V_DIM, :] = (
            vt[hd * V_DIM:(hd + 1) * V_DIM].astype(BF16))
        vt_ref[0, hd * V_ROWS + V_DIM:(hd + 1) * V_ROWS, :] = ones

    a = z[:, C_CONF_A:C_CONF_A + CONF_WIDTH]
    g = z[:, C_CONF_G:C_CONF_G + CONF_WIDTH]
    cv_ref[0, :, 0:CONF_WIDTH] = a * jax.nn.sigmoid(g)
    cv_ref[0, :, CONF_WIDTH:CONF_WIDTH + SC_WIDTH] = (
        z[:, C_SC_C:C_SC_C + SC_WIDTH] * z[:, C_SC_H:C_SC_H + SC_WIDTH])
    cv_ref[0, :, CONF_WIDTH + SC_WIDTH:] = z[:, C_SC_B:C_SC_B + SC_WIDTH]


def _in_proj(xl, xc, mod_l, mod_c, wts, cos_t, sin_t, tm):
    bsz, n_lat, _ = xl.shape
    n_ctx = xc.shape[1]
    n_lt, n_ct = n_lat // tm, n_ctx // tm
    n_all = n_lat + n_ctx
    tok = lambda b, i: (b, i, 0)
    vec = lambda b, i: (b, 0, 0)
    cw = CONF_WIDTH + 2 * SC_WIDTH
    return pl.pallas_call(
        functools.partial(_in_proj_kernel, n_lat_tiles=n_lt),
        grid=(bsz, n_lt + n_ct),
        in_specs=[
            pl.BlockSpec((1, tm, D_MODEL), lambda b, i: (b, jnp.minimum(i, n_lt - 1), 0)),
            pl.BlockSpec((1, tm, D_MODEL), lambda b, i: (b, jnp.maximum(i - n_lt, 0), 0)),
            pl.BlockSpec((1, 1, D_MODEL), vec),
            pl.BlockSpec((1, 1, D_MODEL), vec),
            _const_spec((1, 1, D_MODEL)),
            _const_spec((1, 1, D_MODEL)),
            _const_spec((1, D_MODEL)),
            _const_spec((D_MODEL, IN_PACKED)),
            _const_spec((1, Q_LORA)),
            _const_spec((Q_LORA, 2 * MLA_HEADS * HEAD_PAD)),
            _const_spec((1, KV_LORA)),
            _const_spec((KV_LORA, MLA_HEADS * HEAD_PAD)),
            _const_spec((MLA_WIDTH, KV_LORA)),
            pl.BlockSpec((tm, HEAD_PAD), lambda b, i: (i, 0)),
            pl.BlockSpec((tm, HEAD_PAD), lambda b, i: (i, 0)),
        ],
        out_specs=[
            pl.BlockSpec((1, tm, MLA_HEADS * HEAD_PAD), tok),
            pl.BlockSpec((1, tm, MLA_HEADS * HEAD_PAD), tok),
            pl.BlockSpec((1, MLA_HEADS * V_ROWS, tm), lambda b, i: (b, 0, i)),
            pl.BlockSpec((1, tm, cw), tok),
        ],
        out_shape=[
            jax.ShapeDtypeStruct((bsz, n_all, MLA_HEADS * HEAD_PAD), BF16),
            jax.ShapeDtypeStruct((bsz, n_all, MLA_HEADS * HEAD_PAD), BF16),
            jax.ShapeDtypeStruct((bsz, MLA_HEADS * V_ROWS, n_all), BF16),
            jax.ShapeDtypeStruct((bsz, n_all, cw), F32),
        ],
        compiler_params=pltpu.CompilerParams(
            dimension_semantics=("arbitrary", "arbitrary"), vmem_limit_bytes=VMEM_LIMIT),
        name="in_proj",
    )(xl, xc, mod_l[0], mod_l[1], mod_c[0], mod_c[1], wts["g_pre_mix"], wts["w_in"], wts["g_q"],
      wts["w_q"], wts["g_kv"], wts["w_k"], wts["w_vt"], cos_t, sin_t)


def _attn_kernel(q_ref, k_ref, vt_ref, o_ref, qt_ref, sa_ref, sb_ref, acc_ref, *, tq, tk):
    n_q = q_ref.shape[1] // tq
    n_c = k_ref.shape[1] // tk
    total = n_q * n_c

    for i in range(n_q):
        qt_ref[:, i * tq:(i + 1) * tq] = (
            q_ref[0, i * tq:(i + 1) * tq, :].astype(F32).T.astype(BF16))
    acc_ref[...] = jnp.zeros_like(acc_ref)

    def advance(pos):
        qb, ck = pos
        wrap = ck == n_c - 1
        return jnp.where(wrap, qb + 1, qb), jnp.where(wrap, 0, ck + 1)

    def scores(pos, s_ref):
        qb, ck = pos
        qt = qt_ref[:, pl.ds(pl.multiple_of(qb * tq, tq), tq)]
        kc = k_ref[0, pl.ds(pl.multiple_of(ck * tk, tk), tk), :]
        s = jnp.dot(kc, qt, preferred_element_type=F32)
        s_ref[...] = s
        return jnp.max(s, axis=0, keepdims=True)

    def accumulate(pos, s_ref, s_max, m):
        qb, ck = pos
        m_prev = jnp.where(ck == 0, NEG_BIG, m)
        m_new = jnp.maximum(m_prev, s_max)
        alpha = jnp.exp2(m_prev - m_new)
        p = jnp.exp2(s_ref[...] - m_new).astype(BF16)
        vc = vt_ref[0, :, pl.ds(pl.multiple_of(ck * tk, tk), tk)]
        acc = acc_ref[...] * alpha + jnp.dot(vc, p, preferred_element_type=F32)
        acc_ref[...] = acc
        o_ref[0, :, pl.ds(pl.multiple_of(qb * tq, tq), tq)] = (
            acc[0:V_DIM] * (1.0 / acc[V_DIM:V_DIM + 1]))
        return m_new

    zero = jnp.int32(0)
    pos = (zero, zero)
    m = jnp.full((1, tq), NEG_BIG, F32)
    max_a = scores(pos, sa_ref)

    def body(_, carry):
        q0, c0, max_a, m = carry
        pos0 = (q0, c0)
        pos1 = advance(pos0)
        pos2 = advance(pos1)
        max_b = scores(pos1, sb_ref)
        m = accumulate(pos0, sa_ref, max_a, m)
        max_a = scores(pos2, sa_ref)
        m = accumulate(pos1, sb_ref, max_b, m)
        return pos2[0], pos2[1], max_a, m

    n_pairs = (total - 1) // 2
    if n_pairs > 0:
        q0, c0, max_a, m = lax.fori_loop(0, n_pairs, body, (zero, zero, max_a, m),
                                         unroll=PAIR_UNROLL)
        pos = (q0, c0)
    if total % 2 == 0:
        pos1 = advance(pos)
        max_b = scores(pos1, sb_ref)
        m = accumulate(pos, sa_ref, max_a, m)
        accumulate(pos1, sb_ref, max_b, m)
    else:
        accumulate(pos, sa_ref, max_a, m)


def _attention(q, k, vt, seq, t, blk, tq, tk):
    bsz = q.shape[0]
    return pl.pallas_call(
        functools.partial(_attn_kernel, tq=tq, tk=tk),
        grid=(bsz, MLA_HEADS),
        in_specs=[
            pl.BlockSpec((1, seq, HEAD_PAD), lambda b, h: (b, blk, h)),
            pl.BlockSpec((1, t, HEAD_PAD), lambda b, h: (b, blk, h)),
            pl.BlockSpec((1, V_ROWS, t), lambda b, h: (b, h, blk)),
        ],
        out_specs=pl.BlockSpec((1, V_DIM, seq), lambda b, h: (b, h, 0)),
        out_shape=jax.ShapeDtypeStruct((bsz, MLA_WIDTH, seq), F32),
        scratch_shapes=[
            pltpu.VMEM((HEAD_PAD, seq), BF16),
            pltpu.VMEM((tk, tq), F32),
            pltpu.VMEM((tk, tq), F32),
            pltpu.VMEM((V_ROWS, tq), F32),
        ],
        compiler_params=pltpu.CompilerParams(
            dimension_semantics=("arbitrary", "arbitrary"), vmem_limit_bytes=VMEM_LIMIT),
        name="attention",
    )(q, k, vt)


def _mix_out_kernel(x_ref, ot_ref, cv_ref, prev_ref, next_ref, gate_ref, gbr_ref, wo_ref,
                    cw_ref, cb_ref, lng_ref, lnb_ref, sw_ref, gpost_ref, o_ref,
                    uext_ref, ush_ref, cext_ref):
    tm = x_ref.shape[1]
    i = pl.program_id(1)
    has_prev = (i > 0).astype(F32)
    has_next = (i < pl.num_programs(1) - 1).astype(F32)
    c0, c1, c2 = 0, CONF_WIDTH, CONF_WIDTH + SC_WIDTH

    a = ot_ref[0].T
    an = (a * _rms_scale(a, MLA_WIDTH)) * gbr_ref[:, 0:MLA_WIDTH]
    y = jnp.dot(an.astype(BF16), wo_ref[0:MLA_WIDTH, :], preferred_element_type=F32)

    uext_ref[0:HALO, :] = prev_ref[0, :, c0:c1] * has_prev
    uext_ref[HALO:HALO + tm, :] = cv_ref[0, :, c0:c1]
    uext_ref[HALO + tm:, :] = next_ref[0, :, c0:c1] * has_next
    base = HALO - CONF_K // 2
    n_a = ush_ref.shape[1] // SUBLANES - tm // SUBLANES + 1
    conv = cb_ref[...]
    for r in range(SUBLANES):
        ush_ref[r] = uext_ref[base + r:base + r + ush_ref.shape[1], :]
        for a_i in range(n_a):
            t = SUBLANES * a_i + r
            if t < CONF_K:
                conv = conv + cw_ref[t:t + 1, :] * ush_ref[r, SUBLANES * a_i:SUBLANES * a_i + tm, :]
    mu = jnp.sum(conv, axis=-1, keepdims=True) * (1.0 / CONF_WIDTH)
    d = conv - mu
    ln = (d * _rms_scale(d, CONF_WIDTH)) * lng_ref[...] + lnb_ref[...]
    cf = ln * jax.nn.sigmoid(ln)
    cfn = (cf * _rms_scale(cf, CONF_WIDTH)) * gbr_ref[:, MLA_WIDTH:MLA_WIDTH + CONF_WIDTH]
    y = y + jnp.dot(cfn.astype(BF16), wo_ref[MLA_WIDTH:MLA_WIDTH + CONF_WIDTH, :],
                    preferred_element_type=F32)

    cext_ref[0:SUBLANES, :] = prev_ref[0, HALO - SUBLANES:, c1:c2] * has_prev
    cext_ref[SUBLANES:SUBLANES + tm, :] = cv_ref[0, :, c1:c2]
    cext_ref[SUBLANES + tm:, :] = next_ref[0, 0:SUBLANES, c1:c2] * has_next
    base = SUBLANES - SC_K // 2
    sc = sw_ref[0:1, :] * cext_ref[base:base + tm, :]
    for t in range(1, SC_K):
        sc = sc + sw_ref[t:t + 1, :] * cext_ref[base + t:base + t + tm, :]
    sc = cv_ref[0, :, c2:] * sc
    scn = (sc * _rms_scale(sc, SC_WIDTH)) * gbr_ref[:, MLA_WIDTH + CONF_WIDTH:]
    y = y + jnp.dot(scn.astype(BF16), wo_ref[MLA_WIDTH + CONF_WIDTH:, :],
                    preferred_element_type=F32)

    yn = (y * _rms_scale(y, D_MODEL)) * gpost_ref[...]
    o_ref[0] = x_ref[0] + gate_ref[0] * yn


def _mix_out(x, ot, cv, row0, gate, wts, tm):
    bsz, seq, _ = x.shape
    n_t = seq // tm
    hb = tm // HALO
    t0, h0 = row0 // tm, row0 // HALO
    n_h = cv.shape[1] // HALO
    cw = cv.shape[-1]
    tok = lambda b, i: (b, i, 0)
    return pl.pallas_call(
        _mix_out_kernel,
        grid=(bsz, n_t),
        in_specs=[
            pl.BlockSpec((1, tm, D_MODEL), tok),
            pl.BlockSpec((1, MLA_WIDTH, tm), lambda b, i: (b, 0, i)),
            pl.BlockSpec((1, tm, cw), lambda b, i: (b, i + t0, 0)),
            pl.BlockSpec((1, HALO, cw), lambda b, i: (b, jnp.maximum(h0 + i * hb - 1, 0), 0)),
            pl.BlockSpec((1, HALO, cw),
                         lambda b, i: (b, jnp.minimum(h0 + (i + 1) * hb, n_h - 1), 0)),
            pl.BlockSpec((1, 1, D_MODEL), lambda b, i: (b, 0, 0)),
            _const_spec((1, D_MODEL)),
            _const_spec((D_MODEL, D_MODEL)),
            _const_spec((CONF_K, CONF_WIDTH)),
            _const_spec((1, CONF_WIDTH)),
            _const_spec((1, CONF_WIDTH)),
            _const_spec((1, CONF_WIDTH)),
            _const_spec((SC_K, SC_WIDTH)),
            _const_spec((1, D_MODEL)),
        ],
        out_specs=pl.BlockSpec((1, tm, D_MODEL), tok),
        out_shape=jax.ShapeDtypeStruct((bsz, seq, D_MODEL), F32),
        scratch_shapes=[
            pltpu.VMEM((tm + 2 * HALO, CONF_WIDTH), F32),
            pltpu.VMEM((SUBLANES, tm + SUBLANES * (pl.cdiv(CONF_K, SUBLANES) - 1), CONF_WIDTH), F32),
            pltpu.VMEM((tm + 2 * SUBLANES, SC_WIDTH), F32),
        ],
        compiler_params=pltpu.CompilerParams(
            dimension_semantics=("arbitrary", "arbitrary"), vmem_limit_bytes=VMEM_LIMIT),
        name="mix_out",
    )(x, ot, cv, cv, cv, gate, wts["g_branch"], wts["w_o"], wts["conf_dw_w"], wts["conf_dw_b"],
      wts["conf_ln_g"], wts["conf_ln_b"], wts["sc_dw_w"], wts["g_post_mix"])


def _mlp_kernel(x_ref, shift_ref, scale_ref, gate_ref, gpre_ref, w1_ref, w2_ref, gpost_ref,
                o_ref, *, ff_chunk):
    x = x_ref[0]
    gain = gpre_ref[...] * (1.0 + scale_ref[0])
    h = ((x * _rms_scale(x, D_MODEL)) * gain + shift_ref[0]).astype(BF16)
    y = None
    for j in range(D_FF // ff_chunk):
        lo = j * ff_chunk
        a = jnp.maximum(jnp.dot(h, w1_ref[:, lo:lo + ff_chunk], preferred_element_type=F32), 0.0)
        part = jnp.dot((a * a).astype(BF16), w2_ref[lo:lo + ff_chunk, :],
                       preferred_element_type=F32)
        y = part if y is None else y + part
    yn = (y * _rms_scale(y, D_MODEL)) * gpost_ref[...]
    o_ref[0] = x + gate_ref[0] * yn


def _mlp(x, shift, scale, gate, wts, tm, ff_chunk=1024):
    bsz, seq, _ = x.shape
    tok = lambda b, i: (b, i, 0)
    vec = lambda b, i: (b, 0, 0)
    return pl.pallas_call(
        functools.partial(_mlp_kernel, ff_chunk=ff_chunk),
        grid=(bsz, seq // tm),
        in_specs=[
            pl.BlockSpec((1, tm, D_MODEL), tok),
            pl.BlockSpec((1, 1, D_MODEL), vec),
            pl.BlockSpec((1, 1, D_MODEL), vec),
            pl.BlockSpec((1, 1, D_MODEL), vec),
            _const_spec((1, D_MODEL)),
            pl.BlockSpec((D_MODEL, D_FF), lambda b, i: (0, 0), pipeline_mode=pl.Buffered(1)),
            pl.BlockSpec((D_FF, D_MODEL), lambda b, i: (0, 0), pipeline_mode=pl.Buffered(1)),
            _const_spec((1, D_MODEL)),
        ],
        out_specs=pl.BlockSpec((1, tm, D_MODEL), tok),
        out_shape=jax.ShapeDtypeStruct((bsz, seq, D_MODEL), F32),
        compiler_params=pltpu.CompilerParams(
            dimension_semantics=("arbitrary", "arbitrary"), vmem_limit_bytes=VMEM_LIMIT),
        name="mlp",
    )(x, shift, scale, gate, wts["g_pre_mlp"], wts["w_mlp_in"], wts["w_mlp_out"],
      wts["g_post_mlp"])


def _rotate_half_cols(w):
    hh = AXIS_DIM // 2
    parts = []
    for ax in range(2):
        blk = w[..., ax * AXIS_DIM:(ax + 1) * AXIS_DIM]
        parts.append(jnp.concatenate([-blk[..., hh:], blk[..., :hh]], axis=-1))
    return jnp.concatenate(parts, axis=-1)


def _head_group(nope, rope):
    pad = jnp.zeros(nope.shape[:-1] + (HEAD_PAD - QK_DIM,), nope.dtype)
    return jnp.concatenate([nope, rope, pad], axis=-1)


def _pack_layer(i, w_in, g_q, w_q_b, g_kv, w_kv_b, p):
    win = w_in[i]
    zeros_nope = jnp.zeros((D_MODEL, QK_NOPE), F32)
    w_kr = win[:, Q_LORA + KV_LORA:Q_LORA + KV_LORA + QK_ROPE]
    conf0 = Q_LORA + KV_LORA + QK_ROPE
    sc0 = conf0 + 2 * CONF_WIDTH
    win_packed = jnp.concatenate([
        win[:, 0:Q_LORA],
        win[:, Q_LORA:Q_LORA + KV_LORA],
        _head_group(zeros_nope, w_kr),
        _head_group(zeros_nope, _rotate_half_cols(w_kr)),
        win[:, conf0:sc0],
        win[:, sc0:],
    ], axis=-1)

    wq = w_q_b[i].reshape(Q_LORA, MLA_HEADS, QK_DIM)
    wq_nope, wq_rope = wq[..., :QK_NOPE], wq[..., QK_NOPE:]
    wq_main = _head_group(wq_nope, wq_rope).reshape(Q_LORA, MLA_HEADS * HEAD_PAD)
    wq_rot = _head_group(jnp.zeros_like(wq_nope), _rotate_half_cols(wq_rope)).reshape(
        Q_LORA, MLA_HEADS * HEAD_PAD)

    wkv = w_kv_b[i].reshape(KV_LORA, MLA_HEADS, QK_NOPE + V_DIM)
    wk = _head_group(wkv[..., :QK_NOPE], jnp.zeros((KV_LORA, MLA_HEADS, QK_ROPE), F32)).reshape(
        KV_LORA, MLA_HEADS * HEAD_PAD)
    wv_t = jnp.transpose(wkv[..., QK_NOPE:], (1, 2, 0))

    row = lambda v: v[i][None, :]
    return {
        "g_pre_mix": row(p["g_pre_mix"]),
        "w_in": win_packed.astype(BF16),
        "g_q": row(g_q) * (SM_SCALE * math.log2(math.e)),
        "w_q": jnp.concatenate([wq_main, wq_rot], axis=-1).astype(BF16),
        "g_kv": row(g_kv),
        "w_k": wk.astype(BF16),
        "w_vt": wv_t.reshape(MLA_WIDTH, KV_LORA).astype(BF16),
        "g_branch": row(p["g_branch"]),
        "w_o": p["w_o"][i].astype(BF16),
        "conf_dw_w": p["conf_dw_w"][i],
        "conf_dw_b": row(p["conf_dw_b"]),
        "conf_ln_g": row(p["conf_ln_g"]),
        "conf_ln_b": row(p["conf_ln_b"]),
        "sc_dw_w": p["sc_dw_w"][i],
        "g_post_mix": row(p["g_post_mix"]),
        "g_pre_mlp": row(p["g_pre_mlp"]),
        "w_mlp_in": p["w_mlp_in"][i].astype(BF16),
        "w_mlp_out": p["w_mlp_out"][i].astype(BF16),
        "g_post_mlp": row(p["g_post_mlp"]),
    }


def _rope_tables(n_lat):
    rows = n_lat // GRID_W
    row = jnp.broadcast_to(jnp.arange(rows)[:, None], (rows, GRID_W)).reshape(-1).astype(F32)
    col = jnp.broadcast_to(jnp.arange(GRID_W)[None, :], (rows, GRID_W)).reshape(-1).astype(F32)
    inv = 1.0 / (ROPE_THETA ** (jnp.arange(0, AXIS_DIM, 2, dtype=F32) / AXIS_DIM))
    ar = row[:, None] * inv
    ac = col[:, None] * inv
    ang = jnp.concatenate([ar, ar, ac, ac], axis=-1)
    return (_head_group(jnp.ones((n_lat, QK_NOPE), F32), jnp.cos(ang)),
            _head_group(jnp.zeros((n_lat, QK_NOPE), F32), jnp.sin(ang)))


def _identity_tables(n):
    return (_head_group(jnp.ones((n, QK_NOPE), F32), jnp.ones((n, QK_ROPE), F32)),
            jnp.zeros((n, HEAD_PAD), F32))


def kernel(x, c, ctx, c_ctx, w_mod, b_mod, g_pre_mix, g_post_mix, g_pre_mlp, g_post_mlp, w_in,
           g_q, w_q_b, g_kv, w_kv_b, conf_dw_w, conf_dw_b, conf_ln_g, conf_ln_b, sc_dw_w,
           g_branch, w_o, w_mlp_in, w_mlp_out):
    bsz, n_lat, _ = x.shape
    n_ctx = ctx.shape[1]
    depth = w_mod.shape[0]
    params = dict(g_pre_mix=g_pre_mix, g_post_mix=g_post_mix, g_pre_mlp=g_pre_mlp,
                  g_post_mlp=g_post_mlp, conf_dw_w=conf_dw_w, conf_dw_b=conf_dw_b,
                  conf_ln_g=conf_ln_g, conf_ln_b=conf_ln_b, sc_dw_w=sc_dw_w, g_branch=g_branch,
                  w_o=w_o, w_mlp_in=w_mlp_in, w_mlp_out=w_mlp_out)

    mod_rows = SUBLANES * pl.cdiv(bsz + 1, SUBLANES)
    cvecs = jnp.concatenate(
        [c, c_ctx[None, :], jnp.zeros((mod_rows - bsz - 1, D_MODEL), F32)], axis=0)
    mod = _modulation(cvecs, w_mod, b_mod).reshape(depth, mod_rows, N_MOD, D_MODEL)

    cos_l, sin_l = _rope_tables(n_lat)
    cos_c, sin_c = _identity_tables(n_ctx)
    cos_t = jnp.concatenate([cos_l, cos_c], axis=0)
    sin_t = jnp.concatenate([sin_l, sin_c], axis=0)
    n_all = n_lat + n_ctx
    tm = min(256, n_ctx)
    tq_lat, tk_lat = 512, 768
    assert n_lat % tm == 0 and n_ctx % tm == 0 and n_lat % n_ctx == 0
    assert n_lat % tq_lat == 0 and n_all % tk_lat == 0
    ctx_blk = n_lat // n_ctx

    xl, xc = x, ctx
    for i in range(depth):
        last = i == depth - 1
        wts = _pack_layer(i, w_in, g_q, w_q_b, g_kv, w_kv_b, params)
        ml = [mod[i, :bsz, j][:, None, :] for j in range(N_MOD)]
        mc1 = [mod[i, bsz, j][None, None, :] for j in range(N_MOD)]

        q, k, vt, cv = _in_proj(xl, xc, ml[0:2], mc1[0:2], wts, cos_t, sin_t, tm)

        ot_l = _attention(q, k, vt, n_lat, n_all, 0, tq_lat, tk_lat)
        xl = _mix_out(xl, ot_l, cv, 0, ml[2], wts, tm)
        xl = _mlp(xl, ml[3], ml[4], ml[5], wts, 512)

        if not last:
            mc = [jnp.broadcast_to(v, (bsz, 1, D_MODEL)) for v in mc1]
            ot_c = _attention(q, k, vt, n_ctx, n_ctx, ctx_blk, n_ctx, n_ctx)
            xc = _mix_out(xc, ot_c, cv, n_lat, mc[2], wts, tm)
            xc = _mlp(xc, mc[3], mc[4], mc[5], wts, tm)
    return xl
```

```python
import functools
import math

import jax
import jax.numpy as jnp
from jax import lax
from jax.experimental import pallas as pl
from jax.experimental.pallas import tpu as pltpu

F32 = jnp.float32
BF16 = jnp.bfloat16

D_MODEL = 1024
GRID_W = 64
N_MOD = 6
MLA_HEADS = 8
QK_NOPE = 64
QK_ROPE = 32
QK_DIM = QK_NOPE + QK_ROPE
V_DIM = 64
MLA_WIDTH = MLA_HEADS * V_DIM
Q_LORA = 256
KV_LORA = 128
AXIS_DIM = QK_ROPE // 2
ROPE_THETA = 10000.0
CONF_WIDTH = 256
CONF_K = 31
SC_WIDTH = 256
SC_K = 3
D_FF = 4 * D_MODEL
EPS = 1e-6
SM_SCALE = 1.0 / math.sqrt(QK_DIM)

LANES = 128
SUBLANES = 8
BF16_ROWS = 16
MXU_DIM = 256

HEAD_PAD = LANES
V_ROWS = V_DIM + BF16_ROWS
HALO = 16

C_Q = 0
C_CKV = C_Q + Q_LORA
C_KR = C_CKV + KV_LORA
C_KR_ROT = C_KR + HEAD_PAD
C_CONF_A = C_KR_ROT + HEAD_PAD
C_CONF_G = C_CONF_A + CONF_WIDTH
C_SC_B = C_CONF_G + CONF_WIDTH
C_SC_C = C_SC_B + SC_WIDTH
C_SC_H = C_SC_C + SC_WIDTH
IN_PACKED = C_SC_H + SC_WIDTH

NEG_BIG = -1e30
PAIR_UNROLL = 8
VMEM_LIMIT = 56 * 1024 * 1024


def _const_spec(shape):
    nd = len(shape)
    return pl.BlockSpec(shape, lambda *_: (0,) * nd)


def _rms_scale(v, width):
    return lax.rsqrt(jnp.sum(v * v, axis=-1, keepdims=True) * (1.0 / width) + EPS)


def _mod_kernel(c_ref, w_ref, b_ref, o_ref):
    cv = c_ref[...]
    s = cv * jax.nn.sigmoid(cv)
    w = w_ref[0]
    s_hi = s.astype(BF16)
    s_lo = (s - s_hi.astype(F32)).astype(BF16)
    w_hi = w.astype(BF16)
    w_lo = (w - w_hi.astype(F32)).astype(BF16)
    acc = jnp.dot(s_hi, w_lo, preferred_element_type=F32)
    acc = acc + jnp.dot(s_lo, w_hi, preferred_element_type=F32)
    acc = acc + jnp.dot(s_hi, w_hi, preferred_element_type=F32)
    o_ref[0] = acc + b_ref[0]


def _modulation(cvecs, w_mod, b_mod):
    depth = w_mod.shape[0]
    rows = cvecs.shape[0]
    return pl.pallas_call(
        _mod_kernel,
        grid=(depth, N_MOD),
        in_specs=[
            _const_spec((rows, D_MODEL)),
            pl.BlockSpec((1, D_MODEL, D_MODEL), lambda l, j: (l, 0, j)),
            pl.BlockSpec((1, 1, D_MODEL), lambda l, j: (l, 0, j)),
        ],
        out_specs=pl.BlockSpec((1, rows, D_MODEL), lambda l, j: (l, 0, j)),
        out_shape=jax.ShapeDtypeStruct((depth, rows, N_MOD * D_MODEL), F32),
        compiler_params=pltpu.CompilerParams(
            dimension_semantics=("arbitrary", "arbitrary"), vmem_limit_bytes=VMEM_LIMIT),
        name="modulation",
    )(cvecs, w_mod, b_mod.reshape(depth, 1, N_MOD * D_MODEL))


def _in_proj_kernel(xl_ref, xc_ref, shl_ref, scl_ref, shc_ref, scc_ref, gpre_ref, win_ref, gq_ref,
                    wq_ref, gkv_ref, wk_ref, wvt_ref, cos_ref, sin_ref, q_ref, k_ref, vt_ref, cv_ref,
                    *, n_lat_tiles):
    is_ctx = pl.program_id(1) >= n_lat_tiles
    x = jnp.where(is_ctx, xc_ref[0], xl_ref[0])
    shift = jnp.where(is_ctx, shc_ref[0], shl_ref[0])
    scale = jnp.where(is_ctx, scc_ref[0], scl_ref[0])
    gain = gpre_ref[...] * (1.0 + scale)
    h = (x * _rms_scale(x, D_MODEL)) * gain + shift
    z = jnp.dot(h.astype(BF16), win_ref[...], preferred_element_type=F32)
    cos = cos_ref[...]
    sin = sin_ref[...]

    zq = z[:, C_Q:C_Q + Q_LORA]
    zqn = (zq * _rms_scale(zq, Q_LORA)) * gq_ref[...]
    qq = jnp.dot(zqn.astype(BF16), wq_ref[...], preferred_element_type=F32)
    half = MLA_HEADS * HEAD_PAD
    for hd in range(MLA_HEADS):
        lo = hd * HEAD_PAD
        qh = qq[:, lo:lo + HEAD_PAD] * cos + qq[:, half + lo:half + lo + HEAD_PAD] * sin
        q_ref[0, :, lo:lo + HEAD_PAD] = qh.astype(BF16)

    ckv = z[:, C_CKV:C_CKV + KV_LORA]
    ckvn = ((ckv * _rms_scale(ckv, KV_LORA)) * gkv_ref[...]).astype(BF16)
    kk = jnp.dot(ckvn, wk_ref[...], preferred_element_type=F32)
    kr = z[:, C_KR:C_KR + HEAD_PAD] * cos + z[:, C_KR_ROT:C_KR_ROT + HEAD_PAD] * sin
    for hd in range(MLA_HEADS):
        lo = hd * HEAD_PAD
        k_ref[0, :, lo:lo + HEAD_PAD] = (kk[:, lo:lo + HEAD_PAD] + kr).astype(BF16)

    vt = lax.dot_general(wvt_ref[...], ckvn, (((1,), (1,)), ((), ())),
                         preferred_element_type=F32)
    ones = jnp.ones((V_ROWS - V_DIM, vt.shape[1]), BF16)
    for hd in range(MLA_HEADS):
        vt_ref[0, hd * V_ROWS:hd * V_ROWS + V_DIM, :] = (
            vt[hd * V_DIM:(hd + 1) * V_DIM].astype(BF16))
        vt_ref[0, hd * V_ROWS + V_DIM:(hd + 1) * V_ROWS, :] = ones

    a = z[:, C_CONF_A:C_CONF_A + CONF_WIDTH]
    g = z[:, C_CONF_G:C_CONF_G + CONF_WIDTH]
    cv_ref[0, :, 0:CONF_WIDTH] = a * jax.nn.sigmoid(g)
    cv_ref[0, :, CONF_WIDTH:CONF_WIDTH + SC_WIDTH] = (
        z[:, C_SC_C:C_SC_C + SC_WIDTH] * z[:, C_SC_H:C_SC_H + SC_WIDTH])
    cv_ref[0, :, CONF_WIDTH + SC_WIDTH:] = z[:, C_SC_B:C_SC_B + SC_WIDTH]


def _in_proj(xl, xc, mod_l, mod_c, wts, cos_t, sin_t, tm):
    bsz, n_lat, _ = xl.shape
    n_ctx = xc.shape[1]
    n_lt, n_ct = n_lat // tm, n_ctx // tm
    n_all = n_lat + n_ctx
    tok = lambda b, i: (b, i, 0)
    vec = lambda b, i: (b, 0, 0)
    cw = CONF_WIDTH + 2 * SC_WIDTH
    return pl.pallas_call(
        functools.partial(_in_proj_kernel, n_lat_tiles=n_lt),
        grid=(bsz, n_lt + n_ct),
        in_specs=[
            pl.BlockSpec((1, tm, D_MODEL), lambda b, i: (b, jnp.minimum(i, n_lt - 1), 0)),
            pl.BlockSpec((1, tm, D_MODEL), lambda b, i: (b, jnp.maximum(i - n_lt, 0), 0)),
            pl.BlockSpec((1, 1, D_MODEL), vec),
            pl.BlockSpec((1, 1, D_MODEL), vec),
            _const_spec((1, 1, D_MODEL)),
            _const_spec((1, 1, D_MODEL)),
            _const_spec((1, D_MODEL)),
            _const_spec((D_MODEL, IN_PACKED)),
            _const_spec((1, Q_LORA)),
            _const_spec((Q_LORA, 2 * MLA_HEADS * HEAD_PAD)),
            _const_spec((1, KV_LORA)),
            _const_spec((KV_LORA, MLA_HEADS * HEAD_PAD)),
            _const_spec((MLA_WIDTH, KV_LORA)),
            pl.BlockSpec((tm, HEAD_PAD), lambda b, i: (i, 0)),
            pl.BlockSpec((tm, HEAD_PAD), lambda b, i: (i, 0)),
        ],
        out_specs=[
            pl.BlockSpec((1, tm, MLA_HEADS * HEAD_PAD), tok),
            pl.BlockSpec((1, tm, MLA_HEADS * HEAD_PAD), tok),
            pl.BlockSpec((1, MLA_HEADS * V_ROWS, tm), lambda b, i: (b, 0, i)),
            pl.BlockSpec((1, tm, cw), tok),
        ],
        out_shape=[
            jax.ShapeDtypeStruct((bsz, n_all, MLA_HEADS * HEAD_PAD), BF16),
            jax.ShapeDtypeStruct((bsz, n_all, MLA_HEADS * HEAD_PAD), BF16),
            jax.ShapeDtypeStruct((bsz, MLA_HEADS * V_ROWS, n_all), BF16),
            jax.ShapeDtypeStruct((bsz, n_all, cw), F32),
        ],
        compiler_params=pltpu.CompilerParams(
            dimension_semantics=("arbitrary", "arbitrary"), vmem_limit_bytes=VMEM_LIMIT),
        name="in_proj",
    )(xl, xc, mod_l[0], mod_l[1], mod_c[0], mod_c[1], wts["g_pre_mix"], wts["w_in"], wts["g_q"],
      wts["w_q"], wts["g_kv"], wts["w_k"], wts["w_vt"], cos_t, sin_t)


def _attn_kernel(q_ref, k_ref, vt_ref, o_ref, qt_ref, sa_ref, sb_ref, acc_ref, *, tq, tk):
    n_q = q_ref.shape[1] // tq
    n_c = k_ref.shape[1] // tk
    total = n_q * n_c

    for i in range(n_q):
        qt_ref[:, i * tq:(i + 1) * tq] = (
            q_ref[0, i * tq:(i + 1) * tq, :].astype(F32).T.astype(BF16))
    acc_ref[...] = jnp.zeros_like(acc_ref)

    def advance(pos):
        qb, ck = pos
        wrap = ck == n_c - 1
        return jnp.where(wrap, qb + 1, qb), jnp.where(wrap, 0, ck + 1)

    def scores(pos, s_ref):
        qb, ck = pos
        qt = qt_ref[:, pl.ds(pl.multiple_of(qb * tq, tq), tq)]
        kc = k_ref[0, pl.ds(pl.multiple_of(ck * tk, tk), tk), :]
        s = jnp.dot(kc, qt, preferred_element_type=F32)
        s_ref[...] = s
        return jnp.max(s, axis=0, keepdims=True)

    def accumulate(pos, s_ref, s_max, m):
        qb, ck = pos
        m_prev = jnp.where(ck == 0, NEG_BIG, m)
        m_new = jnp.maximum(m_prev, s_max)
        alpha = jnp.exp2(m_prev - m_new)
        p = jnp.exp2(s_ref[...] - m_new).astype(BF16)
        vc = vt_ref[0, :, pl.ds(pl.multiple_of(ck * tk, tk), tk)]
        acc = acc_ref[...] * alpha + jnp.dot(vc, p, preferred_element_type=F32)
        acc_ref[...] = acc
        o_ref[0, :, pl.ds(pl.multiple_of(qb * tq, tq), tq)] = (
            acc[0:V_DIM] * (1.0 / acc[V_DIM:V_DIM + 1]))
        return m_new

    zero = jnp.int32(0)
    pos = (zero, zero)
    m = jnp.full((1, tq), NEG_BIG, F32)
    max_a = scores(pos, sa_ref)

    def body(_, carry):
        q0, c0, max_a, m = carry
        pos0 = (q0, c0)
        pos1 = advance(pos0)
        pos2 = advance(pos1)
        max_b = scores(pos1, sb_ref)
        m = accumulate(pos0, sa_ref, max_a, m)
        max_a = scores(pos2, sa_ref)
        m = accumulate(pos1, sb_ref, max_b, m)
        return pos2[0], pos2[1], max_a, m

    n_pairs = (total - 1) // 2
    if n_pairs > 0:
        q0, c0, max_a, m = lax.fori_loop(0, n_pairs, body, (zero, zero, max_a, m),
                                         unroll=PAIR_UNROLL)
        pos = (q0, c0)
    if total % 2 == 0:
        pos1 = advance(pos)
        max_b = scores(pos1, sb_ref)
        m = accumulate(pos, sa_ref, max_a, m)
        accumulate(pos1, sb_ref, max_b, m)
    else:
        accumulate(pos, sa_ref, max_a, m)


def _attention(q, k, vt, seq, t, blk, tq, tk):
    bsz = q.shape[0]
    return pl.pallas_call(
        functools.partial(_attn_kernel, tq=tq, tk=tk),
        grid=(bsz, MLA_HEADS),
        in_specs=[
            pl.BlockSpec((1, seq, HEAD_PAD), lambda b, h: (b, blk, h)),
            pl.BlockSpec((1, t, HEAD_PAD), lambda b, h: (b, blk, h)),
            pl.BlockSpec((1, V_ROWS, t), lambda b, h: (b, h, blk)),
        ],
        out_specs=pl.BlockSpec((1, V_DIM, seq), lambda b, h: (b, h, 0)),
        out_shape=jax.ShapeDtypeStruct((bsz, MLA_WIDTH, seq), F32),
        scratch_shapes=[
            pltpu.VMEM((HEAD_PAD, seq), BF16),
            pltpu.VMEM((tk, tq), F32),
            pltpu.VMEM((tk, tq), F32),
            pltpu.VMEM((V_ROWS, tq), F32),
        ],
        compiler_params=pltpu.CompilerParams(
            dimension_semantics=("arbitrary", "arbitrary"), vmem_limit_bytes=VMEM_LIMIT),
        name="attention",
    )(q, k, vt)


def _mix_tile(tile, n_tiles, x_ref, ot_ref, cv_ref, prev_ref, next_ref, gate_ref, gbr_ref, wo_ref,
              cw_ref, cb_ref, lng_ref, lnb_ref, sw_ref, gpost_ref, uext_ref, ush_ref, cext_ref):
    tm = x_ref.shape[1]
    has_prev = (tile > 0).astype(F32)
    has_next = (tile < n_tiles - 1).astype(F32)
    c0, c1, c2 = 0, CONF_WIDTH, CONF_WIDTH + SC_WIDTH

    a = ot_ref[0].T
    an = (a * _rms_scale(a, MLA_WIDTH)) * gbr_ref[:, 0:MLA_WIDTH]
    y = jnp.dot(an.astype(BF16), wo_ref[0:MLA_WIDTH, :], preferred_element_type=F32)

    yield
    uext_ref[0:HALO, :] = prev_ref[0, :, c0:c1] * has_prev
    uext_ref[HALO:HALO + tm, :] = cv_ref[0, :, c0:c1]
    uext_ref[HALO + tm:, :] = next_ref[0, :, c0:c1] * has_next
    base = HALO - CONF_K // 2
    n_a = ush_ref.shape[1] // SUBLANES - tm // SUBLANES + 1
    conv = cb_ref[...]
    for r in range(SUBLANES):
        if r == SUBLANES // 2:
            yield
        ush_ref[r] = uext_ref[base + r:base + r + ush_ref.shape[1], :]
        for a_i in range(n_a):
            t = SUBLANES * a_i + r
            if t < CONF_K:
                conv = conv + cw_ref[t:t + 1, :] * ush_ref[r, SUBLANES * a_i:SUBLANES * a_i + tm, :]
    yield
    mu = jnp.sum(conv, axis=-1, keepdims=True) * (1.0 / CONF_WIDTH)
    d = conv - mu
    ln = (d * _rms_scale(d, CONF_WIDTH)) * lng_ref[...] + lnb_ref[...]
    cf = ln * jax.nn.sigmoid(ln)
    cfn = (cf * _rms_scale(cf, CONF_WIDTH)) * gbr_ref[:, MLA_WIDTH:MLA_WIDTH + CONF_WIDTH]
    y = y + jnp.dot(cfn.astype(BF16), wo_ref[MLA_WIDTH:MLA_WIDTH + CONF_WIDTH, :],
                    preferred_element_type=F32)

    yield
    cext_ref[0:SUBLANES, :] = prev_ref[0, HALO - SUBLANES:, c1:c2] * has_prev
    cext_ref[SUBLANES:SUBLANES + tm, :] = cv_ref[0, :, c1:c2]
    cext_ref[SUBLANES + tm:, :] = next_ref[0, 0:SUBLANES, c1:c2] * has_next
    base = SUBLANES - SC_K // 2
    sc = sw_ref[0:1, :] * cext_ref[base:base + tm, :]
    for t in range(1, SC_K):
        sc = sc + sw_ref[t:t + 1, :] * cext_ref[base + t:base + t + tm, :]
    sc = cv_ref[0, :, c2:] * sc
    scn = (sc * _rms_scale(sc, SC_WIDTH)) * gbr_ref[:, MLA_WIDTH + CONF_WIDTH:]
    y = y + jnp.dot(scn.astype(BF16), wo_ref[MLA_WIDTH + CONF_WIDTH:, :],
                    preferred_element_type=F32)

    yn = (y * _rms_scale(y, D_MODEL)) * gpost_ref[...]
    return x_ref[0] + gate_ref[0] * yn


def _mlp_tile(x, shift_ref, scale_ref, gate_ref, gpre_ref, w1_ref, w2_ref, gpost_ref, ff_chunk):
    gain = gpre_ref[...] * (1.0 + scale_ref[0])
    h = ((x * _rms_scale(x, D_MODEL)) * gain + shift_ref[0]).astype(BF16)
    y = None
    for j in range(D_FF // ff_chunk):
        yield
        lo = j * ff_chunk
        a = jnp.maximum(jnp.dot(h, w1_ref[:, lo:lo + ff_chunk], preferred_element_type=F32), 0.0)
        part = jnp.dot((a * a).astype(BF16), w2_ref[lo:lo + ff_chunk, :],
                       preferred_element_type=F32)
        y = part if y is None else y + part
    yn = (y * _rms_scale(y, D_MODEL)) * gpost_ref[...]
    return x + gate_ref[0] * yn


def _alternate(*stage_generators):
    pending = dict(enumerate(stage_generators))
    results = {}
    while pending:
        for key in list(pending):
            try:
                next(pending[key])
            except StopIteration as done:
                results[key] = done.value
                del pending[key]
    return tuple(results[k] for k in range(len(stage_generators)))


def _post_kernel(x_ref, ot_ref, cv_ref, prev_ref, next_ref, gmix_ref, gbr_ref, wo_ref, cw_ref,
                 cb_ref, lng_ref, lnb_ref, sw_ref, gpmix_ref, shift_ref, scale_ref, gmlp_ref,
                 gpre_ref, w1_ref, w2_ref, gpmlp_ref, o_ref,
                 uext_ref, ush_ref, cext_ref, new_ref, old_ref, *, n_tiles, ff_chunk):
    j = pl.program_id(1)

    @pl.when(j == 0)
    def _():
        old_ref[...] = jnp.zeros_like(old_ref)

    mix = _mix_tile(jnp.minimum(j, n_tiles - 1), n_tiles, x_ref, ot_ref, cv_ref, prev_ref,
                    next_ref, gmix_ref, gbr_ref, wo_ref, cw_ref, cb_ref, lng_ref, lnb_ref, sw_ref,
                    gpmix_ref, uext_ref, ush_ref, cext_ref)
    mlp = _mlp_tile(old_ref[...], shift_ref, scale_ref, gmlp_ref, gpre_ref, w1_ref, w2_ref,
                    gpmlp_ref, ff_chunk)
    mixed, out = _alternate(mix, mlp)
    new_ref[...] = mixed
    o_ref[0] = out
    old_ref[...] = new_ref[...]


def _post(x, ot, cv, row0, mods, wts, tm, ff_chunk=1024):
    bsz, seq, _ = x.shape
    n_t = seq // tm
    hb = tm // HALO
    t0, h0 = row0 // tm, row0 // HALO
    n_h = cv.shape[1] // HALO
    cw = cv.shape[-1]
    tile = lambda i: jnp.minimum(i, n_t - 1)
    vec = lambda b, i: (b, 0, 0)
    single = lambda shape: pl.BlockSpec(shape, lambda b, i: (0,) * len(shape),
                                        pipeline_mode=pl.Buffered(1))
    return pl.pallas_call(
        functools.partial(_post_kernel, n_tiles=n_t, ff_chunk=ff_chunk),
        grid=(bsz, n_t + 1),
        in_specs=[
            pl.BlockSpec((1, tm, D_MODEL), lambda b, i: (b, tile(i), 0)),
            pl.BlockSpec((1, MLA_WIDTH, tm), lambda b, i: (b, 0, tile(i))),
            pl.BlockSpec((1, tm, cw), lambda b, i: (b, tile(i) + t0, 0)),
            pl.BlockSpec((1, HALO, cw),
                         lambda b, i: (b, jnp.maximum(h0 + tile(i) * hb - 1, 0), 0)),
            pl.BlockSpec((1, HALO, cw),
                         lambda b, i: (b, jnp.minimum(h0 + (tile(i) + 1) * hb, n_h - 1), 0)),
            pl.BlockSpec((1, 1, D_MODEL), vec),
            _const_spec((1, D_MODEL)),
            single((D_MODEL, D_MODEL)),
            _const_spec((CONF_K, CONF_WIDTH)),
            _const_spec((1, CONF_WIDTH)),
            _const_spec((1, CONF_WIDTH)),
            _const_spec((1, CONF_WIDTH)),
            _const_spec((SC_K, SC_WIDTH)),
            _const_spec((1, D_MODEL)),
            pl.BlockSpec((1, 1, D_MODEL), vec),
            pl.BlockSpec((1, 1, D_MODEL), vec),
            pl.BlockSpec((1, 1, D_MODEL), vec),
            _const_spec((1, D_MODEL)),
            single((D_MODEL, D_FF)),
            single((D_FF, D_MODEL)),
            _const_spec((1, D_MODEL)),
        ],
        out_specs=pl.BlockSpec((1, tm, D_MODEL), lambda b, i: (b, jnp.maximum(i - 1, 0), 0)),
        out_shape=jax.ShapeDtypeStruct((bsz, seq, D_MODEL), F32),
        scratch_shapes=[
            pltpu.VMEM((tm + 2 * HALO, CONF_WIDTH), F32),
            pltpu.VMEM((SUBLANES, tm + SUBLANES * (pl.cdiv(CONF_K, SUBLANES) - 1), CONF_WIDTH), F32),
            pltpu.VMEM((tm + 2 * SUBLANES, SC_WIDTH), F32),
            pltpu.VMEM((tm, D_MODEL), F32),
            pltpu.VMEM((tm, D_MODEL), F32),
        ],
        compiler_params=pltpu.CompilerParams(
            dimension_semantics=("arbitrary", "arbitrary"), vmem_limit_bytes=VMEM_LIMIT),
        name="post",
    )(x, ot, cv, cv, cv, mods[0], wts["g_branch"], wts["w_o"], wts["conf_dw_w"], wts["conf_dw_b"],
      wts["conf_ln_g"], wts["conf_ln_b"], wts["sc_dw_w"], wts["g_post_mix"], mods[1], mods[2],
      mods[3], wts["g_pre_mlp"], wts["w_mlp_in"], wts["w_mlp_out"], wts["g_post_mlp"])


def _rotate_half_cols(w):
    hh = AXIS_DIM // 2
    parts = []
    for ax in range(2):
        blk = w[..., ax * AXIS_DIM:(ax + 1) * AXIS_DIM]
        parts.append(jnp.concatenate([-blk[..., hh:], blk[..., :hh]], axis=-1))
    return jnp.concatenate(parts, axis=-1)


def _head_group(nope, rope):
    pad = jnp.zeros(nope.shape[:-1] + (HEAD_PAD - QK_DIM,), nope.dtype)
    return jnp.concatenate([nope, rope, pad], axis=-1)


def _pack_layer(i, w_in, g_q, w_q_b, g_kv, w_kv_b, p):
    win = w_in[i]
    zeros_nope = jnp.zeros((D_MODEL, QK_NOPE), F32)
    w_kr = win[:, Q_LORA + KV_LORA:Q_LORA + KV_LORA + QK_ROPE]
    conf0 = Q_LORA + KV_LORA + QK_ROPE
    sc0 = conf0 + 2 * CONF_WIDTH
    win_packed = jnp.concatenate([
        win[:, 0:Q_LORA],
        win[:, Q_LORA:Q_LORA + KV_LORA],
        _head_group(zeros_nope, w_kr),
        _head_group(zeros_nope, _rotate_half_cols(w_kr)),
        win[:, conf0:sc0],
        win[:, sc0:],
    ], axis=-1)

    wq = w_q_b[i].reshape(Q_LORA, MLA_HEADS, QK_DIM)
    wq_nope, wq_rope = wq[..., :QK_NOPE], wq[..., QK_NOPE:]
    wq_main = _head_group(wq_nope, wq_rope).reshape(Q_LORA, MLA_HEADS * HEAD_PAD)
    wq_rot = _head_group(jnp.zeros_like(wq_nope), _rotate_half_cols(wq_rope)).reshape(
        Q_LORA, MLA_HEADS * HEAD_PAD)

    wkv = w_kv_b[i].reshape(KV_LORA, MLA_HEADS, QK_NOPE + V_DIM)
    wk = _head_group(wkv[..., :QK_NOPE], jnp.zeros((KV_LORA, MLA_HEADS, QK_ROPE), F32)).reshape(
        KV_LORA, MLA_HEADS * HEAD_PAD)
    wv_t = jnp.transpose(wkv[..., QK_NOPE:], (1, 2, 0))

    row = lambda v: v[i][None, :]
    return {
        "g_pre_mix": row(p["g_pre_mix"]),
        "w_in": win_packed.astype(BF16),
        "g_q": row(g_q) * (SM_SCALE * math.log2(math.e)),
        "w_q": jnp.concatenate([wq_main, wq_rot], axis=-1).astype(BF16),
        "g_kv": row(g_kv),
        "w_k": wk.astype(BF16),
        "w_vt": wv_t.reshape(MLA_WIDTH, KV_LORA).astype(BF16),
        "g_branch": row(p["g_branch"]),
        "w_o": p["w_o"][i].astype(BF16),
        "conf_dw_w": p["conf_dw_w"][i],
        "conf_dw_b": row(p["conf_dw_b"]),
        "conf_ln_g": row(p["conf_ln_g"]),
        "conf_ln_b": row(p["conf_ln_b"]),
        "sc_dw_w": p["sc_dw_w"][i],
        "g_post_mix": row(p["g_post_mix"]),
        "g_pre_mlp": row(p["g_pre_mlp"]),
        "w_mlp_in": p["w_mlp_in"][i].astype(BF16),
        "w_mlp_out": p["w_mlp_out"][i].astype(BF16),
        "g_post_mlp": row(p["g_post_mlp"]),
    }


def _rope_tables(n_lat):
    rows = n_lat // GRID_W
    row = jnp.broadcast_to(jnp.arange(rows)[:, None], (rows, GRID_W)).reshape(-1).astype(F32)
    col = jnp.broadcast_to(jnp.arange(GRID_W)[None, :], (rows, GRID_W)).reshape(-1).astype(F32)
    inv = 1.0 / (ROPE_THETA ** (jnp.arange(0, AXIS_DIM, 2, dtype=F32) / AXIS_DIM))
    ar = row[:, None] * inv
    ac = col[:, None] * inv
    ang = jnp.concatenate([ar, ar, ac, ac], axis=-1)
    return (_head_group(jnp.ones((n_lat, QK_NOPE), F32), jnp.cos(ang)),
            _head_group(jnp.zeros((n_lat, QK_NOPE), F32), jnp.sin(ang)))


def _identity_tables(n):
    return (_head_group(jnp.ones((n, QK_NOPE), F32), jnp.ones((n, QK_ROPE), F32)),
            jnp.zeros((n, HEAD_PAD), F32))


def kernel(x, c, ctx, c_ctx, w_mod, b_mod, g_pre_mix, g_post_mix, g_pre_mlp, g_post_mlp, w_in,
           g_q, w_q_b, g_kv, w_kv_b, conf_dw_w, conf_dw_b, conf_ln_g, conf_ln_b, sc_dw_w,
           g_branch, w_o, w_mlp_in, w_mlp_out):
    bsz, n_lat, _ = x.shape
    n_ctx = ctx.shape[1]
    depth = w_mod.shape[0]
    params = dict(g_pre_mix=g_pre_mix, g_post_mix=g_post_mix, g_pre_mlp=g_pre_mlp,
                  g_post_mlp=g_post_mlp, conf_dw_w=conf_dw_w, conf_dw_b=conf_dw_b,
                  conf_ln_g=conf_ln_g, conf_ln_b=conf_ln_b, sc_dw_w=sc_dw_w, g_branch=g_branch,
                  w_o=w_o, w_mlp_in=w_mlp_in, w_mlp_out=w_mlp_out)

    mod_rows = SUBLANES * pl.cdiv(bsz + 1, SUBLANES)
    cvecs = jnp.concatenate(
        [c, c_ctx[None, :], jnp.zeros((mod_rows - bsz - 1, D_MODEL), F32)], axis=0)
    mod = _modulation(cvecs, w_mod, b_mod).reshape(depth, mod_rows, N_MOD, D_MODEL)

    cos_l, sin_l = _rope_tables(n_lat)
    cos_c, sin_c = _identity_tables(n_ctx)
    cos_t = jnp.concatenate([cos_l, cos_c], axis=0)
    sin_t = jnp.concatenate([sin_l, sin_c], axis=0)
    n_all = n_lat + n_ctx
    tm = min(256, n_ctx)
    tq_lat, tk_lat = 512, 768
    assert n_lat % tm == 0 and n_ctx % tm == 0 and n_lat % n_ctx == 0
    tm_post = 256
    assert n_lat % tq_lat == 0 and n_all % tk_lat == 0 and n_lat % tm_post == 0
    ctx_blk = n_lat // n_ctx

    xl, xc = x, ctx
    for i in range(depth):
        last = i == depth - 1
        wts = _pack_layer(i, w_in, g_q, w_q_b, g_kv, w_kv_b, params)
        ml = [mod[i, :bsz, j][:, None, :] for j in range(N_MOD)]
        mc1 = [mod[i, bsz, j][None, None, :] for j in range(N_MOD)]

        q, k, vt, cv = _in_proj(xl, xc, ml[0:2], mc1[0:2], wts, cos_t, sin_t, tm)

        ot_l = _attention(q, k, vt, n_lat, n_all, 0, tq_lat, tk_lat)
        xl = _post(xl, ot_l, cv, 0, ml[2:6], wts, tm_post)

        if not last:
            mc = [jnp.broadcast_to(v, (bsz, 1, D_MODEL)) for v in mc1]
            ot_c = _attention(q, k, vt, n_ctx, n_ctx, ctx_blk, n_ctx, n_ctx)
            xc = _post(xc, ot_c, cv, n_lat, mc[2:6], wts, tm)
    return xl
```

```python
import functools
import math

import jax
import jax.numpy as jnp
from jax import lax
from jax.experimental import pallas as pl
from jax.experimental.pallas import tpu as pltpu

F32 = jnp.float32
BF16 = jnp.bfloat16

D_MODEL = 1024
GRID_W = 64
N_MOD = 6
MLA_HEADS = 8
QK_NOPE = 64
QK_ROPE = 32
QK_DIM = QK_NOPE + QK_ROPE
V_DIM = 64
MLA_WIDTH = MLA_HEADS * V_DIM
Q_LORA = 256
KV_LORA = 128
AXIS_DIM = QK_ROPE // 2
ROPE_THETA = 10000.0
CONF_WIDTH = 256
CONF_K = 31
SC_WIDTH = 256
SC_K = 3
D_FF = 4 * D_MODEL
EPS = 1e-6
SM_SCALE = 1.0 / math.sqrt(QK_DIM)

LANES = 128
SUBLANES = 8
BF16_ROWS = 16
MXU_DIM = 256

HEAD_PAD = LANES
V_ROWS = V_DIM + BF16_ROWS
HALO = 16

C_Q = 0
C_CKV = C_Q + Q_LORA
C_KR = C_CKV + KV_LORA
C_KR_ROT = C_KR + HEAD_PAD
C_CONF_A = C_KR_ROT + HEAD_PAD
C_CONF_G = C_CONF_A + CONF_WIDTH
C_SC_B = C_CONF_G + CONF_WIDTH
C_SC_C = C_SC_B + SC_WIDTH
C_SC_H = C_SC_C + SC_WIDTH
IN_PACKED = C_SC_H + SC_WIDTH

NEG_BIG = -1e30
PAIR_UNROLL = 8
SCORE_ROW_PAD = 2 * LANES
VMEM_LIMIT = 56 * 1024 * 1024


def _const_spec(shape):
    nd = len(shape)
    return pl.BlockSpec(shape, lambda *_: (0,) * nd)


def _rms_scale(v, width):
    return lax.rsqrt(jnp.sum(v * v, axis=-1, keepdims=True) * (1.0 / width) + EPS)


def _mod_kernel(c_ref, w_ref, b_ref, o_ref):
    cv = c_ref[...]
    s = cv * jax.nn.sigmoid(cv)
    w = w_ref[0]
    s_hi = s.astype(BF16)
    s_lo = (s - s_hi.astype(F32)).astype(BF16)
    w_hi = w.astype(BF16)
    w_lo = (w - w_hi.astype(F32)).astype(BF16)
    acc = jnp.dot(s_hi, w_lo, preferred_element_type=F32)
    acc = acc + jnp.dot(s_lo, w_hi, preferred_element_type=F32)
    acc = acc + jnp.dot(s_hi, w_hi, preferred_element_type=F32)
    o_ref[0] = acc + b_ref[0]


def _modulation(cvecs, w_mod, b_mod):
    depth = w_mod.shape[0]
    rows = cvecs.shape[0]
    return pl.pallas_call(
        _mod_kernel,
        grid=(depth, N_MOD),
        in_specs=[
            _const_spec((rows, D_MODEL)),
            pl.BlockSpec((1, D_MODEL, D_MODEL), lambda l, j: (l, 0, j)),
            pl.BlockSpec((1, 1, D_MODEL), lambda l, j: (l, 0, j)),
        ],
        out_specs=pl.BlockSpec((1, rows, D_MODEL), lambda l, j: (l, 0, j)),
        out_shape=jax.ShapeDtypeStruct((depth, rows, N_MOD * D_MODEL), F32),
        compiler_params=pltpu.CompilerParams(
            dimension_semantics=("arbitrary", "arbitrary"), vmem_limit_bytes=VMEM_LIMIT),
        name="modulation",
    )(cvecs, w_mod, b_mod.reshape(depth, 1, N_MOD * D_MODEL))


def _in_proj_kernel(xl_ref, xc_ref, shl_ref, scl_ref, shc_ref, scc_ref, gpre_ref, win_ref, gq_ref,
                    wqt_ref, gkv_ref, wk_ref, wvt_ref, cos_ref, sin_ref, cost_ref, sint_ref,
                    q_ref, k_ref, vt_ref, cv_ref,
                    *, n_lat_tiles):
    is_ctx = pl.program_id(1) >= n_lat_tiles
    x = jnp.where(is_ctx, xc_ref[0], xl_ref[0])
    shift = jnp.where(is_ctx, shc_ref[0], shl_ref[0])
    scale = jnp.where(is_ctx, scc_ref[0], scl_ref[0])
    gain = gpre_ref[...] * (1.0 + scale)
    h = (x * _rms_scale(x, D_MODEL)) * gain + shift
    z = jnp.dot(h.astype(BF16), win_ref[...], preferred_element_type=F32)
    cos = cos_ref[...]
    sin = sin_ref[...]

    zq = z[:, C_Q:C_Q + Q_LORA]
    zqn = (zq * _rms_scale(zq, Q_LORA)) * gq_ref[...]
    qq = lax.dot_general(wqt_ref[...], zqn.astype(BF16), (((1,), (1,)), ((), ())),
                         preferred_element_type=F32)
    cos_q = cost_ref[...]
    sin_q = sint_ref[...]
    half = MLA_HEADS * HEAD_PAD
    for hd in range(MLA_HEADS):
        lo = hd * HEAD_PAD
        qh = qq[lo:lo + HEAD_PAD] * cos_q + qq[half + lo:half + lo + HEAD_PAD] * sin_q
        q_ref[0, lo:lo + HEAD_PAD, :] = qh.astype(BF16)

    ckv = z[:, C_CKV:C_CKV + KV_LORA]
    ckvn = ((ckv * _rms_scale(ckv, KV_LORA)) * gkv_ref[...]).astype(BF16)
    kk = jnp.dot(ckvn, wk_ref[...], preferred_element_type=F32)
    kr = z[:, C_KR:C_KR + HEAD_PAD] * cos + z[:, C_KR_ROT:C_KR_ROT + HEAD_PAD] * sin
    for hd in range(MLA_HEADS):
        lo = hd * HEAD_PAD
        k_ref[0, :, lo:lo + HEAD_PAD] = (kk[:, lo:lo + HEAD_PAD] + kr).astype(BF16)

    vt = lax.dot_general(wvt_ref[...], ckvn, (((1,), (1,)), ((), ())),
                         preferred_element_type=F32)
    ones = jnp.ones((V_ROWS - V_DIM, vt.shape[1]), BF16)
    for hd in range(MLA_HEADS):
        vt_ref[0, hd * V_ROWS:hd * V_ROWS + V_DIM, :] = (
            vt[hd * V_DIM:(hd + 1) * V_DIM].astype(BF16))
        vt_ref[0, hd * V_ROWS + V_DIM:(hd + 1) * V_ROWS, :] = ones

    a = z[:, C_CONF_A:C_CONF_A + CONF_WIDTH]
    g = z[:, C_CONF_G:C_CONF_G + CONF_WIDTH]
    cv_ref[0, :, 0:CONF_WIDTH] = a * jax.nn.sigmoid(g)
    cv_ref[0, :, CONF_WIDTH:CONF_WIDTH + SC_WIDTH] = (
        z[:, C_SC_C:C_SC_C + SC_WIDTH] * z[:, C_SC_H:C_SC_H + SC_WIDTH])
    cv_ref[0, :, CONF_WIDTH + SC_WIDTH:] = z[:, C_SC_B:C_SC_B + SC_WIDTH]


def _in_proj(xl, xc, mod_l, mod_c, wts, cos_t, sin_t, tm):
    bsz, n_lat, _ = xl.shape
    n_ctx = xc.shape[1]
    n_lt, n_ct = n_lat // tm, n_ctx // tm
    n_all = n_lat + n_ctx
    tok = lambda b, i: (b, i, 0)
    vec = lambda b, i: (b, 0, 0)
    cw = CONF_WIDTH + 2 * SC_WIDTH
    return pl.pallas_call(
        functools.partial(_in_proj_kernel, n_lat_tiles=n_lt),
        grid=(bsz, n_lt + n_ct),
        in_specs=[
            pl.BlockSpec((1, tm, D_MODEL), lambda b, i: (b, jnp.minimum(i, n_lt - 1), 0)),
            pl.BlockSpec((1, tm, D_MODEL), lambda b, i: (b, jnp.maximum(i - n_lt, 0), 0)),
            pl.BlockSpec((1, 1, D_MODEL), vec),
            pl.BlockSpec((1, 1, D_MODEL), vec),
            _const_spec((1, 1, D_MODEL)),
            _const_spec((1, 1, D_MODEL)),
            _const_spec((1, D_MODEL)),
            _const_spec((D_MODEL, IN_PACKED)),
            _const_spec((1, Q_LORA)),
            _const_spec((2 * MLA_HEADS * HEAD_PAD, Q_LORA)),
            _const_spec((1, KV_LORA)),
            _const_spec((KV_LORA, MLA_HEADS * HEAD_PAD)),
            _const_spec((MLA_WIDTH, KV_LORA)),
            pl.BlockSpec((tm, HEAD_PAD), lambda b, i: (i, 0)),
            pl.BlockSpec((tm, HEAD_PAD), lambda b, i: (i, 0)),
            pl.BlockSpec((HEAD_PAD, tm), lambda b, i: (0, i)),
            pl.BlockSpec((HEAD_PAD, tm), lambda b, i: (0, i)),
        ],
        out_specs=[
            pl.BlockSpec((1, MLA_HEADS * HEAD_PAD, tm), lambda b, i: (b, 0, i)),
            pl.BlockSpec((1, tm, MLA_HEADS * HEAD_PAD), tok),
            pl.BlockSpec((1, MLA_HEADS * V_ROWS, tm), lambda b, i: (b, 0, i)),
            pl.BlockSpec((1, tm, cw), tok),
        ],
        out_shape=[
            jax.ShapeDtypeStruct((bsz, MLA_HEADS * HEAD_PAD, n_all), BF16),
            jax.ShapeDtypeStruct((bsz, n_all, MLA_HEADS * HEAD_PAD), BF16),
            jax.ShapeDtypeStruct((bsz, MLA_HEADS * V_ROWS, n_all), BF16),
            jax.ShapeDtypeStruct((bsz, n_all, cw), F32),
        ],
        compiler_params=pltpu.CompilerParams(
            dimension_semantics=("arbitrary", "arbitrary"), vmem_limit_bytes=VMEM_LIMIT),
        name="in_proj",
    )(xl, xc, mod_l[0], mod_l[1], mod_c[0], mod_c[1], wts["g_pre_mix"], wts["w_in"], wts["g_q"],
      wts["w_q_t"], wts["g_kv"], wts["w_k"], wts["w_vt"], cos_t, sin_t, cos_t.T, sin_t.T)


def _attn_kernel(qt_ref, k_ref, vt_ref, o_ref, sa_ref, sb_ref, acc_ref, *, tq, tk):
    n_q = qt_ref.shape[2] // tq
    n_c = k_ref.shape[1] // tk
    total = n_q * n_c
    acc_ref[...] = jnp.zeros_like(acc_ref)

    def advance(pos):
        qb, ck = pos
        wrap = ck == n_c - 1
        return jnp.where(wrap, qb + 1, qb), jnp.where(wrap, 0, ck + 1)

    def scores(pos, s_ref):
        qb, ck = pos
        qt = qt_ref[0, :, pl.ds(pl.multiple_of(qb * tq, tq), tq)]
        kc = k_ref[0, pl.ds(pl.multiple_of(ck * tk, tk), tk), :]
        s = jnp.dot(kc, qt, preferred_element_type=F32)
        s_ref[:, 0:tq] = s
        return jnp.max(s, axis=0, keepdims=True)

    def accumulate(pos, s_ref, s_max, m):
        qb, ck = pos
        m_prev = jnp.where(ck == 0, NEG_BIG, m)
        m_new = jnp.maximum(m_prev, s_max)
        alpha = jnp.exp2(m_prev - m_new)
        p = jnp.exp2(s_ref[:, 0:tq] - m_new).astype(BF16)
        vc = vt_ref[0, :, pl.ds(pl.multiple_of(ck * tk, tk), tk)]
        acc = acc_ref[...] * alpha + jnp.dot(vc, p, preferred_element_type=F32)
        acc_ref[...] = acc
        o_ref[0, :, pl.ds(pl.multiple_of(qb * tq, tq), tq)] = (
            acc[0:V_DIM] * (1.0 / acc[V_DIM:V_DIM + 1]))
        return m_new

    zero = jnp.int32(0)
    pos = (zero, zero)
    m = jnp.full((1, tq), NEG_BIG, F32)
    max_a = scores(pos, sa_ref)

    def body(_, carry):
        q0, c0, max_a, m = carry
        pos0 = (q0, c0)
        pos1 = advance(pos0)
        pos2 = advance(pos1)
        max_b = scores(pos1, sb_ref)
        m = accumulate(pos0, sa_ref, max_a, m)
        max_a = scores(pos2, sa_ref)
        m = accumulate(pos1, sb_ref, max_b, m)
        return pos2[0], pos2[1], max_a, m

    n_pairs = (total - 1) // 2
    if n_pairs > 0:
        q0, c0, max_a, m = lax.fori_loop(0, n_pairs, body, (zero, zero, max_a, m),
                                         unroll=PAIR_UNROLL)
        pos = (q0, c0)
    if total % 2 == 0:
        pos1 = advance(pos)
        max_b = scores(pos1, sb_ref)
        m = accumulate(pos, sa_ref, max_a, m)
        accumulate(pos1, sb_ref, max_b, m)
    else:
        accumulate(pos, sa_ref, max_a, m)


def _attention(q, k, vt, seq, t, blk, tq, tk):
    bsz = q.shape[0]
    return pl.pallas_call(
        functools.partial(_attn_kernel, tq=tq, tk=tk),
        grid=(bsz, MLA_HEADS),
        in_specs=[
            pl.BlockSpec((1, HEAD_PAD, seq), lambda b, h: (b, h, blk)),
            pl.BlockSpec((1, t, HEAD_PAD), lambda b, h: (b, blk, h)),
            pl.BlockSpec((1, V_ROWS, t), lambda b, h: (b, h, blk)),
        ],
        out_specs=pl.BlockSpec((1, V_DIM, seq), lambda b, h: (b, h, 0)),
        out_shape=jax.ShapeDtypeStruct((bsz, MLA_WIDTH, seq), F32),
        scratch_shapes=[
            pltpu.VMEM((tk, tq + SCORE_ROW_PAD), F32),
            pltpu.VMEM((tk, tq + SCORE_ROW_PAD), F32),
            pltpu.VMEM((V_ROWS, tq), F32),
        ],
        compiler_params=pltpu.CompilerParams(
            dimension_semantics=("arbitrary", "arbitrary"), vmem_limit_bytes=VMEM_LIMIT),
        name="attention",
    )(q, k, vt)


def _mix_tile(tile, n_tiles, x_ref, ot_ref, cv_ref, prev_ref, next_ref, gate_ref, gbr_ref, wo_ref,
              cw_ref, cb_ref, lng_ref, lnb_ref, sw_ref, gpost_ref, uext_ref, ush_ref, cext_ref):
    tm = x_ref.shape[1]
    has_prev = jnp.where(tile > 0, 1.0, 0.0)
    has_next = jnp.where(tile < n_tiles - 1, 1.0, 0.0)
    c0, c1, c2 = 0, CONF_WIDTH, CONF_WIDTH + SC_WIDTH

    a = ot_ref[0].T
    an = (a * _rms_scale(a, MLA_WIDTH)) * gbr_ref[:, 0:MLA_WIDTH]
    y = jnp.dot(an.astype(BF16), wo_ref[0:MLA_WIDTH, :], preferred_element_type=F32)

    yield
    uext_ref[0:HALO, :] = prev_ref[0, :, c0:c1] * has_prev
    uext_ref[HALO:HALO + tm, :] = cv_ref[0, :, c0:c1]
    uext_ref[HALO + tm:, :] = next_ref[0, :, c0:c1] * has_next
    base = HALO - CONF_K // 2
    n_a = ush_ref.shape[1] // SUBLANES - tm // SUBLANES + 1
    conv = cb_ref[...]
    for r in range(SUBLANES):
        if r == SUBLANES // 2:
            yield
        ush_ref[r] = uext_ref[base + r:base + r + ush_ref.shape[1], :]
        for a_i in range(n_a):
            t = SUBLANES * a_i + r
            if t < CONF_K:
                conv = conv + cw_ref[t:t + 1, :] * ush_ref[r, SUBLANES * a_i:SUBLANES * a_i + tm, :]
    yield
    mu = jnp.sum(conv, axis=-1, keepdims=True) * (1.0 / CONF_WIDTH)
    d = conv - mu
    ln = (d * _rms_scale(d, CONF_WIDTH)) * lng_ref[...] + lnb_ref[...]
    cf = ln * jax.nn.sigmoid(ln)
    cfn = (cf * _rms_scale(cf, CONF_WIDTH)) * gbr_ref[:, MLA_WIDTH:MLA_WIDTH + CONF_WIDTH]
    y = y + jnp.dot(cfn.astype(BF16), wo_ref[MLA_WIDTH:MLA_WIDTH + CONF_WIDTH, :],
                    preferred_element_type=F32)

    yield
    cext_ref[0:SUBLANES, :] = prev_ref[0, HALO - SUBLANES:, c1:c2] * has_prev
    cext_ref[SUBLANES:SUBLANES + tm, :] = cv_ref[0, :, c1:c2]
    cext_ref[SUBLANES + tm:, :] = next_ref[0, 0:SUBLANES, c1:c2] * has_next
    base = SUBLANES - SC_K // 2
    sc = sw_ref[0:1, :] * cext_ref[base:base + tm, :]
    for t in range(1, SC_K):
        sc = sc + sw_ref[t:t + 1, :] * cext_ref[base + t:base + t + tm, :]
    sc = cv_ref[0, :, c2:] * sc
    scn = (sc * _rms_scale(sc, SC_WIDTH)) * gbr_ref[:, MLA_WIDTH + CONF_WIDTH:]
    y = y + jnp.dot(scn.astype(BF16), wo_ref[MLA_WIDTH + CONF_WIDTH:, :],
                    preferred_element_type=F32)

    yn = (y * _rms_scale(y, D_MODEL)) * gpost_ref[...]
    return x_ref[0] + gate_ref[0] * yn


def _mlp_tile(x, shift_ref, scale_ref, gate_ref, gpre_ref, w1_ref, w2_ref, gpost_ref, ff_chunk):
    gain = gpre_ref[...] * (1.0 + scale_ref[0])
    h = ((x * _rms_scale(x, D_MODEL)) * gain + shift_ref[0]).astype(BF16)
    y = None
    for j in range(D_FF // ff_chunk):
        yield
        lo = j * ff_chunk
        a = jnp.maximum(jnp.dot(h, w1_ref[:, lo:lo + ff_chunk], preferred_element_type=F32), 0.0)
        part = jnp.dot((a * a).astype(BF16), w2_ref[lo:lo + ff_chunk, :],
                       preferred_element_type=F32)
        y = part if y is None else y + part
    yn = (y * _rms_scale(y, D_MODEL)) * gpost_ref[...]
    return x + gate_ref[0] * yn


def _alternate(*stage_generators):
    pending = dict(enumerate(stage_generators))
    results = {}
    while pending:
        for key in list(pending):
            try:
                next(pending[key])
            except StopIteration as done:
                results[key] = done.value
                del pending[key]
    return tuple(results[k] for k in range(len(stage_generators)))


def _post_kernel(x_ref, ot_ref, cv_ref, prev_ref, next_ref, gmix_ref, gbr_ref, wo_ref, cw_ref,
                 cb_ref, lng_ref, lnb_ref, sw_ref, gpmix_ref, shift_ref, scale_ref, gmlp_ref,
                 gpre_ref, w1_ref, w2_ref, gpmlp_ref, o_ref,
                 uext_ref, ush_ref, cext_ref, new_ref, old_ref, *, n_tiles, ff_chunk):
    j = pl.program_id(1)
    if n_tiles == 1:
        (mixed,) = _alternate(_mix_tile(
            0, 1, x_ref, ot_ref, cv_ref, prev_ref, next_ref, gmix_ref, gbr_ref, wo_ref, cw_ref,
            cb_ref, lng_ref, lnb_ref, sw_ref, gpmix_ref, uext_ref, ush_ref, cext_ref))
        (o_ref[0],) = _alternate(_mlp_tile(mixed, shift_ref, scale_ref, gmlp_ref, gpre_ref,
                                           w1_ref, w2_ref, gpmlp_ref, ff_chunk))
        return

    @pl.when(j == 0)
    def _():
        old_ref[...] = jnp.zeros_like(old_ref)

    mix = _mix_tile(jnp.minimum(j, n_tiles - 1), n_tiles, x_ref, ot_ref, cv_ref, prev_ref,
                    next_ref, gmix_ref, gbr_ref, wo_ref, cw_ref, cb_ref, lng_ref, lnb_ref, sw_ref,
                    gpmix_ref, uext_ref, ush_ref, cext_ref)
    mlp = _mlp_tile(old_ref[...], shift_ref, scale_ref, gmlp_ref, gpre_ref, w1_ref, w2_ref,
                    gpmlp_ref, ff_chunk)
    mixed, out = _alternate(mix, mlp)
    new_ref[...] = mixed
    o_ref[0] = out
    old_ref[...] = new_ref[...]


def _post(x, ot, cv, row0, mods, wts, tm, ff_chunk=1024):
    bsz, seq, _ = x.shape
    n_t = seq // tm
    hb = tm // HALO
    t0, h0 = row0 // tm, row0 // HALO
    n_h = cv.shape[1] // HALO
    cw = cv.shape[-1]
    lag = 1 if n_t > 1 else 0
    tile = lambda i: jnp.minimum(i, n_t - 1)
    vec = lambda b, i: (b, 0, 0)
    single = lambda shape: pl.BlockSpec(shape, lambda b, i: (0,) * len(shape),
                                        pipeline_mode=pl.Buffered(1))
    return pl.pallas_call(
        functools.partial(_post_kernel, n_tiles=n_t, ff_chunk=ff_chunk),
        grid=(bsz, n_t + lag),
        in_specs=[
            pl.BlockSpec((1, tm, D_MODEL), lambda b, i: (b, tile(i), 0)),
            pl.BlockSpec((1, MLA_WIDTH, tm), lambda b, i: (b, 0, tile(i))),
            pl.BlockSpec((1, tm, cw), lambda b, i: (b, tile(i) + t0, 0)),
            pl.BlockSpec((1, HALO, cw),
                         lambda b, i: (b, jnp.maximum(h0 + tile(i) * hb - 1, 0), 0)),
            pl.BlockSpec((1, HALO, cw),
                         lambda b, i: (b, jnp.minimum(h0 + (tile(i) + 1) * hb, n_h - 1), 0)),
            pl.BlockSpec((1, 1, D_MODEL), vec),
            _const_spec((1, D_MODEL)),
            single((D_MODEL, D_MODEL)),
            _const_spec((CONF_K, CONF_WIDTH)),
            _const_spec((1, CONF_WIDTH)),
            _const_spec((1, CONF_WIDTH)),
            _const_spec((1, CONF_WIDTH)),
            _const_spec((SC_K, SC_WIDTH)),
            _const_spec((1, D_MODEL)),
            pl.BlockSpec((1, 1, D_MODEL), vec),
            pl.BlockSpec((1, 1, D_MODEL), vec),
            pl.BlockSpec((1, 1, D_MODEL), vec),
            _const_spec((1, D_MODEL)),
            single((D_MODEL, D_FF)),
            single((D_FF, D_MODEL)),
            _const_spec((1, D_MODEL)),
        ],
        out_specs=pl.BlockSpec((1, tm, D_MODEL), lambda b, i: (b, jnp.maximum(i - lag, 0), 0)),
        out_shape=jax.ShapeDtypeStruct((bsz, seq, D_MODEL), F32),
        scratch_shapes=[
            pltpu.VMEM((tm + 2 * HALO, CONF_WIDTH), F32),
            pltpu.VMEM((SUBLANES, tm + SUBLANES * (pl.cdiv(CONF_K, SUBLANES) - 1), CONF_WIDTH), F32),
            pltpu.VMEM((tm + 2 * SUBLANES, SC_WIDTH), F32),
            pltpu.VMEM((tm, D_MODEL), F32),
            pltpu.VMEM((tm, D_MODEL), F32),
        ],
        compiler_params=pltpu.CompilerParams(
            dimension_semantics=("arbitrary", "arbitrary"), vmem_limit_bytes=VMEM_LIMIT),
        name="post",
    )(x, ot, cv, cv, cv, mods[0], wts["g_branch"], wts["w_o"], wts["conf_dw_w"], wts["conf_dw_b"],
      wts["conf_ln_g"], wts["conf_ln_b"], wts["sc_dw_w"], wts["g_post_mix"], mods[1], mods[2],
      mods[3], wts["g_pre_mlp"], wts["w_mlp_in"], wts["w_mlp_out"], wts["g_post_mlp"])


def _rotate_half_cols(w):
    hh = AXIS_DIM // 2
    parts = []
    for ax in range(2):
        blk = w[..., ax * AXIS_DIM:(ax + 1) * AXIS_DIM]
        parts.append(jnp.concatenate([-blk[..., hh:], blk[..., :hh]], axis=-1))
    return jnp.concatenate(parts, axis=-1)


def _head_group(nope, rope):
    pad = jnp.zeros(nope.shape[:-1] + (HEAD_PAD - QK_DIM,), nope.dtype)
    return jnp.concatenate([nope, rope, pad], axis=-1)


def _pack_layer(i, w_in, g_q, w_q_b, g_kv, w_kv_b, p):
    win = w_in[i]
    zeros_nope = jnp.zeros((D_MODEL, QK_NOPE), F32)
    w_kr = win[:, Q_LORA + KV_LORA:Q_LORA + KV_LORA + QK_ROPE]
    conf0 = Q_LORA + KV_LORA + QK_ROPE
    sc0 = conf0 + 2 * CONF_WIDTH
    win_packed = jnp.concatenate([
        win[:, 0:Q_LORA],
        win[:, Q_LORA:Q_LORA + KV_LORA],
        _head_group(zeros_nope, w_kr),
        _head_group(zeros_nope, _rotate_half_cols(w_kr)),
        win[:, conf0:sc0],
        win[:, sc0:],
    ], axis=-1)

    wq = w_q_b[i].reshape(Q_LORA, MLA_HEADS, QK_DIM)
    wq_nope, wq_rope = wq[..., :QK_NOPE], wq[..., QK_NOPE:]
    wq_main = _head_group(wq_nope, wq_rope).reshape(Q_LORA, MLA_HEADS * HEAD_PAD)
    wq_rot = _head_group(jnp.zeros_like(wq_nope), _rotate_half_cols(wq_rope)).reshape(
        Q_LORA, MLA_HEADS * HEAD_PAD)

    wkv = w_kv_b[i].reshape(KV_LORA, MLA_HEADS, QK_NOPE + V_DIM)
    wk = _head_group(wkv[..., :QK_NOPE], jnp.zeros((KV_LORA, MLA_HEADS, QK_ROPE), F32)).reshape(
        KV_LORA, MLA_HEADS * HEAD_PAD)
    wv_t = jnp.transpose(wkv[..., QK_NOPE:], (1, 2, 0))

    row = lambda v: v[i][None, :]
    return {
        "g_pre_mix": row(p["g_pre_mix"]),
        "w_in": win_packed.astype(BF16),
        "g_q": row(g_q) * (SM_SCALE * math.log2(math.e)),
        "w_q_t": jnp.concatenate([wq_main, wq_rot], axis=-1).T.astype(BF16),
        "g_kv": row(g_kv),
        "w_k": wk.astype(BF16),
        "w_vt": wv_t.reshape(MLA_WIDTH, KV_LORA).astype(BF16),
        "g_branch": row(p["g_branch"]),
        "w_o": p["w_o"][i].astype(BF16),
        "conf_dw_w": p["conf_dw_w"][i],
        "conf_dw_b": row(p["conf_dw_b"]),
        "conf_ln_g": row(p["conf_ln_g"]),
        "conf_ln_b": row(p["conf_ln_b"]),
        "sc_dw_w": p["sc_dw_w"][i],
        "g_post_mix": row(p["g_post_mix"]),
        "g_pre_mlp": row(p["g_pre_mlp"]),
        "w_mlp_in": p["w_mlp_in"][i].astype(BF16),
        "w_mlp_out": p["w_mlp_out"][i].astype(BF16),
        "g_post_mlp": row(p["g_post_mlp"]),
    }


def _rope_tables(n_lat):
    rows = n_lat // GRID_W
    row = jnp.broadcast_to(jnp.arange(rows)[:, None], (rows, GRID_W)).reshape(-1).astype(F32)
    col = jnp.broadcast_to(jnp.arange(GRID_W)[None, :], (rows, GRID_W)).reshape(-1).astype(F32)
    inv = 1.0 / (ROPE_THETA ** (jnp.arange(0, AXIS_DIM, 2, dtype=F32) / AXIS_DIM))
    ar = row[:, None] * inv
    ac = col[:, None] * inv
    ang = jnp.concatenate([ar, ar, ac, ac], axis=-1)
    return (_head_group(jnp.ones((n_lat, QK_NOPE), F32), jnp.cos(ang)),
            _head_group(jnp.zeros((n_lat, QK_NOPE), F32), jnp.sin(ang)))


def _identity_tables(n):
    return (_head_group(jnp.ones((n, QK_NOPE), F32), jnp.ones((n, QK_ROPE), F32)),
            jnp.zeros((n, HEAD_PAD), F32))


def kernel(x, c, ctx, c_ctx, w_mod, b_mod, g_pre_mix, g_post_mix, g_pre_mlp, g_post_mlp, w_in,
           g_q, w_q_b, g_kv, w_kv_b, conf_dw_w, conf_dw_b, conf_ln_g, conf_ln_b, sc_dw_w,
           g_branch, w_o, w_mlp_in, w_mlp_out):
    bsz, n_lat, _ = x.shape
    n_ctx = ctx.shape[1]
    depth = w_mod.shape[0]
    params = dict(g_pre_mix=g_pre_mix, g_post_mix=g_post_mix, g_pre_mlp=g_pre_mlp,
                  g_post_mlp=g_post_mlp, conf_dw_w=conf_dw_w, conf_dw_b=conf_dw_b,
                  conf_ln_g=conf_ln_g, conf_ln_b=conf_ln_b, sc_dw_w=sc_dw_w, g_branch=g_branch,
                  w_o=w_o, w_mlp_in=w_mlp_in, w_mlp_out=w_mlp_out)

    mod_rows = SUBLANES * pl.cdiv(bsz + 1, SUBLANES)
    cvecs = jnp.concatenate(
        [c, c_ctx[None, :], jnp.zeros((mod_rows - bsz - 1, D_MODEL), F32)], axis=0)
    mod = _modulation(cvecs, w_mod, b_mod).reshape(depth, mod_rows, N_MOD, D_MODEL)

    cos_l, sin_l = _rope_tables(n_lat)
    cos_c, sin_c = _identity_tables(n_ctx)
    cos_t = jnp.concatenate([cos_l, cos_c], axis=0)
    sin_t = jnp.concatenate([sin_l, sin_c], axis=0)
    n_all = n_lat + n_ctx
    tm = min(256, n_ctx)
    tq_lat, tk_lat = 512, 768
    assert n_lat % tm == 0 and n_ctx % tm == 0 and n_lat % n_ctx == 0
    tm_post = 256
    assert n_lat % tq_lat == 0 and n_all % tk_lat == 0 and n_lat % tm_post == 0
    ctx_blk = n_lat // n_ctx

    xl, xc = x, ctx
    for i in range(depth):
        last = i == depth - 1
        wts = _pack_layer(i, w_in, g_q, w_q_b, g_kv, w_kv_b, params)
        ml = [mod[i, :bsz, j][:, None, :] for j in range(N_MOD)]
        mc1 = [mod[i, bsz, j][None, None, :] for j in range(N_MOD)]

        q, k, vt, cv = _in_proj(xl, xc, ml[0:2], mc1[0:2], wts, cos_t, sin_t, tm)

        ot_l = _attention(q, k, vt, n_lat, n_all, 0, tq_lat, tk_lat)
        xl = _post(xl, ot_l, cv, 0, ml[2:6], wts, tm_post)

        if not last:
            mc = [jnp.broadcast_to(v, (bsz, 1, D_MODEL)) for v in mc1]
            ot_c = _attention(q, k, vt, n_ctx, n_ctx, ctx_blk, n_ctx, n_ctx)
            xc = _post(xc, ot_c, cv, n_lat, mc[2:6], wts, tm)
    return xl
```

```python
import functools
import math

import jax
import jax.numpy as jnp
from jax import lax
from jax.experimental import pallas as pl
from jax.experimental.pallas import tpu as pltpu

F32 = jnp.float32
BF16 = jnp.bfloat16

D_MODEL = 1024
GRID_W = 64
N_MOD = 6
MLA_HEADS = 8
QK_NOPE = 64
QK_ROPE = 32
QK_DIM = QK_NOPE + QK_ROPE
V_DIM = 64
MLA_WIDTH = MLA_HEADS * V_DIM
Q_LORA = 256
KV_LORA = 128
AXIS_DIM = QK_ROPE // 2
ROPE_THETA = 10000.0
CONF_WIDTH = 256
CONF_K = 31
SC_WIDTH = 256
SC_K = 3
D_FF = 4 * D_MODEL
EPS = 1e-6
SM_SCALE = 1.0 / math.sqrt(QK_DIM)

LANES = 128
SUBLANES = 8
BF16_ROWS = 16
MXU_DIM = 256

HEAD_PAD = LANES
V_ROWS = V_DIM + BF16_ROWS
HALO = 16

C_Q = 0
C_CKV = C_Q + Q_LORA
C_KR = C_CKV + KV_LORA
C_KR_ROT = C_KR + HEAD_PAD
C_CONF_A = C_KR_ROT + HEAD_PAD
C_CONF_G = C_CONF_A + CONF_WIDTH
C_SC_B = C_CONF_G + CONF_WIDTH
C_SC_C = C_SC_B + SC_WIDTH
C_SC_H = C_SC_C + SC_WIDTH
IN_PACKED = C_SC_H + SC_WIDTH

NEG_BIG = -1e30
PAIR_UNROLL = 8
HEADS_PER_STEP = 2
SCORE_ROW_PAD = 2 * LANES
VMEM_LIMIT = 56 * 1024 * 1024


def _const_spec(shape):
    nd = len(shape)
    return pl.BlockSpec(shape, lambda *_: (0,) * nd)


def _rms_scale(v, width):
    return lax.rsqrt(jnp.sum(v * v, axis=-1, keepdims=True) * (1.0 / width) + EPS)


def _mod_kernel(c_ref, w_ref, b_ref, o_ref):
    cv = c_ref[...]
    s = cv * jax.nn.sigmoid(cv)
    w = w_ref[0]
    s_hi = s.astype(BF16)
    s_lo = (s - s_hi.astype(F32)).astype(BF16)
    w_hi = w.astype(BF16)
    w_lo = (w - w_hi.astype(F32)).astype(BF16)
    acc = jnp.dot(s_hi, w_lo, preferred_element_type=F32)
    acc = acc + jnp.dot(s_lo, w_hi, preferred_element_type=F32)
    acc = acc + jnp.dot(s_hi, w_hi, preferred_element_type=F32)
    o_ref[0] = acc + b_ref[0]


def _modulation(cvecs, w_mod, b_mod):
    depth = w_mod.shape[0]
    rows = cvecs.shape[0]
    return pl.pallas_call(
        _mod_kernel,
        grid=(depth, N_MOD),
        in_specs=[
            _const_spec((rows, D_MODEL)),
            pl.BlockSpec((1, D_MODEL, D_MODEL), lambda l, j: (l, 0, j)),
            pl.BlockSpec((1, 1, D_MODEL), lambda l, j: (l, 0, j)),
        ],
        out_specs=pl.BlockSpec((1, rows, D_MODEL), lambda l, j: (l, 0, j)),
        out_shape=jax.ShapeDtypeStruct((depth, rows, N_MOD * D_MODEL), F32),
        compiler_params=pltpu.CompilerParams(
            dimension_semantics=("arbitrary", "arbitrary"), vmem_limit_bytes=VMEM_LIMIT),
        name="modulation",
    )(cvecs, w_mod, b_mod.reshape(depth, 1, N_MOD * D_MODEL))


def _in_proj_kernel(xl_ref, xc_ref, shl_ref, scl_ref, shc_ref, scc_ref, gpre_ref, win_ref, gq_ref,
                    wqt_ref, gkv_ref, wk_ref, wvt_ref, cos_ref, sin_ref, cost_ref, sint_ref,
                    q_ref, k_ref, vt_ref, cv_ref,
                    *, n_lat_tiles):
    is_ctx = pl.program_id(1) >= n_lat_tiles
    x = jnp.where(is_ctx, xc_ref[0], xl_ref[0])
    shift = jnp.where(is_ctx, shc_ref[0], shl_ref[0])
    scale = jnp.where(is_ctx, scc_ref[0], scl_ref[0])
    gain = gpre_ref[...] * (1.0 + scale)
    h = (x * _rms_scale(x, D_MODEL)) * gain + shift
    z = jnp.dot(h.astype(BF16), win_ref[...], preferred_element_type=F32)
    cos = cos_ref[...]
    sin = sin_ref[...]

    zq = z[:, C_Q:C_Q + Q_LORA]
    zqn = (zq * _rms_scale(zq, Q_LORA)) * gq_ref[...]
    qq = lax.dot_general(wqt_ref[...], zqn.astype(BF16), (((1,), (1,)), ((), ())),
                         preferred_element_type=F32)
    cos_q = cost_ref[...]
    sin_q = sint_ref[...]
    half = MLA_HEADS * HEAD_PAD
    for hd in range(MLA_HEADS):
        lo = hd * HEAD_PAD
        qh = qq[lo:lo + HEAD_PAD] * cos_q + qq[half + lo:half + lo + HEAD_PAD] * sin_q
        q_ref[0, lo:lo + HEAD_PAD, :] = qh.astype(BF16)

    ckv = z[:, C_CKV:C_CKV + KV_LORA]
    ckvn = ((ckv * _rms_scale(ckv, KV_LORA)) * gkv_ref[...]).astype(BF16)
    kk = jnp.dot(ckvn, wk_ref[...], preferred_element_type=F32)
    kr = z[:, C_KR:C_KR + HEAD_PAD] * cos + z[:, C_KR_ROT:C_KR_ROT + HEAD_PAD] * sin
    for hd in range(MLA_HEADS):
        lo = hd * HEAD_PAD
        k_ref[0, :, lo:lo + HEAD_PAD] = (kk[:, lo:lo + HEAD_PAD] + kr).astype(BF16)

    vt = lax.dot_general(wvt_ref[...], ckvn, (((1,), (1,)), ((), ())),
                         preferred_element_type=F32)
    ones = jnp.ones((V_ROWS - V_DIM, vt.shape[1]), BF16)
    for hd in range(MLA_HEADS):
        vt_ref[0, hd * V_ROWS:hd * V_ROWS + V_DIM, :] = (
            vt[hd * V_DIM:(hd + 1) * V_DIM].astype(BF16))
        vt_ref[0, hd * V_ROWS + V_DIM:(hd + 1) * V_ROWS, :] = ones

    a = z[:, C_CONF_A:C_CONF_A + CONF_WIDTH]
    g = z[:, C_CONF_G:C_CONF_G + CONF_WIDTH]
    cv_ref[0, :, 0:CONF_WIDTH] = a * jax.nn.sigmoid(g)
    cv_ref[0, :, CONF_WIDTH:CONF_WIDTH + SC_WIDTH] = (
        z[:, C_SC_C:C_SC_C + SC_WIDTH] * z[:, C_SC_H:C_SC_H + SC_WIDTH])
    cv_ref[0, :, CONF_WIDTH + SC_WIDTH:] = z[:, C_SC_B:C_SC_B + SC_WIDTH]


def _in_proj(xl, xc, mod_l, mod_c, wts, cos_t, sin_t, tm):
    bsz, n_lat, _ = xl.shape
    n_ctx = xc.shape[1]
    n_lt, n_ct = n_lat // tm, n_ctx // tm
    n_all = n_lat + n_ctx
    tok = lambda b, i: (b, i, 0)
    vec = lambda b, i: (b, 0, 0)
    cw = CONF_WIDTH + 2 * SC_WIDTH
    return pl.pallas_call(
        functools.partial(_in_proj_kernel, n_lat_tiles=n_lt),
        grid=(bsz, n_lt + n_ct),
        in_specs=[
            pl.BlockSpec((1, tm, D_MODEL), lambda b, i: (b, jnp.minimum(i, n_lt - 1), 0)),
            pl.BlockSpec((1, tm, D_MODEL), lambda b, i: (b, jnp.maximum(i - n_lt, 0), 0)),
            pl.BlockSpec((1, 1, D_MODEL), vec),
            pl.BlockSpec((1, 1, D_MODEL), vec),
            _const_spec((1, 1, D_MODEL)),
            _const_spec((1, 1, D_MODEL)),
            _const_spec((1, D_MODEL)),
            _const_spec((D_MODEL, IN_PACKED)),
            _const_spec((1, Q_LORA)),
            _const_spec((2 * MLA_HEADS * HEAD_PAD, Q_LORA)),
            _const_spec((1, KV_LORA)),
            _const_spec((KV_LORA, MLA_HEADS * HEAD_PAD)),
            _const_spec((MLA_WIDTH, KV_LORA)),
            pl.BlockSpec((tm, HEAD_PAD), lambda b, i: (i, 0)),
            pl.BlockSpec((tm, HEAD_PAD), lambda b, i: (i, 0)),
            pl.BlockSpec((HEAD_PAD, tm), lambda b, i: (0, i)),
            pl.BlockSpec((HEAD_PAD, tm), lambda b, i: (0, i)),
        ],
        out_specs=[
            pl.BlockSpec((1, MLA_HEADS * HEAD_PAD, tm), lambda b, i: (b, 0, i)),
            pl.BlockSpec((1, tm, MLA_HEADS * HEAD_PAD), tok),
            pl.BlockSpec((1, MLA_HEADS * V_ROWS, tm), lambda b, i: (b, 0, i)),
            pl.BlockSpec((1, tm, cw), tok),
        ],
        out_shape=[
            jax.ShapeDtypeStruct((bsz, MLA_HEADS * HEAD_PAD, n_all), BF16),
            jax.ShapeDtypeStruct((bsz, n_all, MLA_HEADS * HEAD_PAD), BF16),
            jax.ShapeDtypeStruct((bsz, MLA_HEADS * V_ROWS, n_all), BF16),
            jax.ShapeDtypeStruct((bsz, n_all, cw), F32),
        ],
        compiler_params=pltpu.CompilerParams(
            dimension_semantics=("arbitrary", "arbitrary"), vmem_limit_bytes=VMEM_LIMIT),
        name="in_proj",
    )(xl, xc, mod_l[0], mod_l[1], mod_c[0], mod_c[1], wts["g_pre_mix"], wts["w_in"], wts["g_q"],
      wts["w_q_t"], wts["g_kv"], wts["w_k"], wts["w_vt"], cos_t, sin_t, cos_t.T, sin_t.T)


def _attn_kernel(qt_ref, k_ref, vt_ref, o_ref, sa_ref, sb_ref, acc_ref, *, tq, tk):
    n_h = qt_ref.shape[1] // HEAD_PAD
    n_q = qt_ref.shape[2] // tq
    n_c = k_ref.shape[1] // tk
    total = n_h * n_q * n_c
    acc_ref[...] = jnp.zeros_like(acc_ref)

    def advance(pos):
        hd, qb, ck = pos
        wrap_c = ck == n_c - 1
        wrap_q = jnp.logical_and(wrap_c, qb == n_q - 1)
        return (jnp.where(wrap_q, hd + 1, hd),
                jnp.where(wrap_q, 0, jnp.where(wrap_c, qb + 1, qb)),
                jnp.where(wrap_c, 0, ck + 1))

    def scores(pos, s_ref):
        hd, qb, ck = pos
        h0 = pl.multiple_of(hd * HEAD_PAD, HEAD_PAD)
        qt = qt_ref[0, pl.ds(h0, HEAD_PAD), pl.ds(pl.multiple_of(qb * tq, tq), tq)]
        kc = k_ref[0, pl.ds(pl.multiple_of(ck * tk, tk), tk), pl.ds(h0, HEAD_PAD)]
        s = jnp.dot(kc, qt, preferred_element_type=F32)
        s_ref[:, 0:tq] = s
        return jnp.max(s, axis=0, keepdims=True)

    def accumulate(pos, s_ref, s_max, m):
        hd, qb, ck = pos
        m_prev = jnp.where(ck == 0, NEG_BIG, m)
        m_new = jnp.maximum(m_prev, s_max)
        alpha = jnp.exp2(m_prev - m_new)
        p = jnp.exp2(s_ref[:, 0:tq] - m_new).astype(BF16)
        vc = vt_ref[0, pl.ds(pl.multiple_of(hd * V_ROWS, BF16_ROWS), V_ROWS),
                    pl.ds(pl.multiple_of(ck * tk, tk), tk)]
        acc = acc_ref[...] * alpha + jnp.dot(vc, p, preferred_element_type=F32)
        acc_ref[...] = acc
        o_ref[0, pl.ds(pl.multiple_of(hd * V_DIM, SUBLANES), V_DIM),
              pl.ds(pl.multiple_of(qb * tq, tq), tq)] = acc[0:V_DIM] * (1.0 / acc[V_DIM:V_DIM + 1])
        return m_new

    zero = jnp.int32(0)
    pos = (zero, zero, zero)
    m = jnp.full((1, tq), NEG_BIG, F32)
    max_a = scores(pos, sa_ref)

    def body(_, carry):
        h0, q0, c0, max_a, m = carry
        pos0 = (h0, q0, c0)
        pos1 = advance(pos0)
        pos2 = advance(pos1)
        max_b = scores(pos1, sb_ref)
        m = accumulate(pos0, sa_ref, max_a, m)
        max_a = scores(pos2, sa_ref)
        m = accumulate(pos1, sb_ref, max_b, m)
        return pos2 + (max_a, m)

    n_pairs = (total - 1) // 2
    if n_pairs > 0:
        carry = lax.fori_loop(0, n_pairs, body, pos + (max_a, m), unroll=PAIR_UNROLL)
        pos, max_a, m = carry[0:3], carry[3], carry[4]
    if total % 2 == 0:
        pos1 = advance(pos)
        max_b = scores(pos1, sb_ref)
        m = accumulate(pos, sa_ref, max_a, m)
        accumulate(pos1, sb_ref, max_b, m)
    else:
        accumulate(pos, sa_ref, max_a, m)


def _attention(q, k, vt, seq, t, blk, tq, tk):
    bsz = q.shape[0]
    return pl.pallas_call(
        functools.partial(_attn_kernel, tq=tq, tk=tk),
        grid=(bsz, MLA_HEADS // HEADS_PER_STEP),
        in_specs=[
            pl.BlockSpec((1, HEADS_PER_STEP * HEAD_PAD, seq), lambda b, h: (b, h, blk)),
            pl.BlockSpec((1, t, HEADS_PER_STEP * HEAD_PAD), lambda b, h: (b, blk, h)),
            pl.BlockSpec((1, HEADS_PER_STEP * V_ROWS, t), lambda b, h: (b, h, blk)),
        ],
        out_specs=pl.BlockSpec((1, HEADS_PER_STEP * V_DIM, seq), lambda b, h: (b, h, 0)),
        out_shape=jax.ShapeDtypeStruct((bsz, MLA_WIDTH, seq), F32),
        scratch_shapes=[
            pltpu.VMEM((tk, tq + SCORE_ROW_PAD), F32),
            pltpu.VMEM((tk, tq + SCORE_ROW_PAD), F32),
            pltpu.VMEM((V_ROWS, tq), F32),
        ],
        compiler_params=pltpu.CompilerParams(
            dimension_semantics=("arbitrary", "arbitrary"), vmem_limit_bytes=VMEM_LIMIT),
        name="attention",
    )(q, k, vt)


def _mix_tile(tile, n_tiles, x_ref, ot_ref, cv_ref, prev_ref, next_ref, gate_ref, gbr_ref, wo_ref,
              cw_ref, cb_ref, lng_ref, lnb_ref, sw_ref, gpost_ref, uext_ref, ush_ref, cext_ref):
    tm = x_ref.shape[1]
    has_prev = jnp.where(tile > 0, 1.0, 0.0)
    has_next = jnp.where(tile < n_tiles - 1, 1.0, 0.0)
    c0, c1, c2 = 0, CONF_WIDTH, CONF_WIDTH + SC_WIDTH

    a = ot_ref[0].T
    an = (a * _rms_scale(a, MLA_WIDTH)) * gbr_ref[:, 0:MLA_WIDTH]
    y = jnp.dot(an.astype(BF16), wo_ref[0:MLA_WIDTH, :], preferred_element_type=F32)

    yield
    uext_ref[0:HALO, :] = prev_ref[0, :, c0:c1] * has_prev
    uext_ref[HALO:HALO + tm, :] = cv_ref[0, :, c0:c1]
    uext_ref[HALO + tm:, :] = next_ref[0, :, c0:c1] * has_next
    base = HALO - CONF_K // 2
    n_a = ush_ref.shape[1] // SUBLANES - tm // SUBLANES + 1
    conv = cb_ref[...]
    for r in range(SUBLANES):
        if r == SUBLANES // 2:
            yield
        ush_ref[r] = uext_ref[base + r:base + r + ush_ref.shape[1], :]
        for a_i in range(n_a):
            t = SUBLANES * a_i + r
            if t < CONF_K:
                conv = conv + cw_ref[t:t + 1, :] * ush_ref[r, SUBLANES * a_i:SUBLANES * a_i + tm, :]
    yield
    mu = jnp.sum(conv, axis=-1, keepdims=True) * (1.0 / CONF_WIDTH)
    d = conv - mu
    ln = (d * _rms_scale(d, CONF_WIDTH)) * lng_ref[...] + lnb_ref[...]
    cf = ln * jax.nn.sigmoid(ln)
    cfn = (cf * _rms_scale(cf, CONF_WIDTH)) * gbr_ref[:, MLA_WIDTH:MLA_WIDTH + CONF_WIDTH]
    y = y + jnp.dot(cfn.astype(BF16), wo_ref[MLA_WIDTH:MLA_WIDTH + CONF_WIDTH, :],
                    preferred_element_type=F32)

    yield
    cext_ref[0:SUBLANES, :] = prev_ref[0, HALO - SUBLANES:, c1:c2] * has_prev
    cext_ref[SUBLANES:SUBLANES + tm, :] = cv_ref[0, :, c1:c2]
    cext_ref[SUBLANES + tm:, :] = next_ref[0, 0:SUBLANES, c1:c2] * has_next
    base = SUBLANES - SC_K // 2
    sc = sw_ref[0:1, :] * cext_ref[base:base + tm, :]
    for t in range(1, SC_K):
        sc = sc + sw_ref[t:t + 1, :] * cext_ref[base + t:base + t + tm, :]
    sc = cv_ref[0, :, c2:] * sc
    scn = (sc * _rms_scale(sc, SC_WIDTH)) * gbr_ref[:, MLA_WIDTH + CONF_WIDTH:]
    y = y + jnp.dot(scn.astype(BF16), wo_ref[MLA_WIDTH + CONF_WIDTH:, :],
                    preferred_element_type=F32)

    yn = (y * _rms_scale(y, D_MODEL)) * gpost_ref[...]
    return x_ref[0] + gate_ref[0] * yn


def _mlp_tile(x, shift_ref, scale_ref, gate_ref, gpre_ref, w1_ref, w2_ref, gpost_ref, ff_chunk):
    gain = gpre_ref[...] * (1.0 + scale_ref[0])
    h = ((x * _rms_scale(x, D_MODEL)) * gain + shift_ref[0]).astype(BF16)
    y = None
    for j in range(D_FF // ff_chunk):
        yield
        lo = j * ff_chunk
        a = jnp.maximum(jnp.dot(h, w1_ref[:, lo:lo + ff_chunk], preferred_element_type=F32), 0.0)
        part = jnp.dot((a * a).astype(BF16), w2_ref[lo:lo + ff_chunk, :],
                       preferred_element_type=F32)
        y = part if y is None else y + part
    yn = (y * _rms_scale(y, D_MODEL)) * gpost_ref[...]
    return x + gate_ref[0] * yn


def _alternate(*stage_generators):
    pending = dict(enumerate(stage_generators))
    results = {}
    while pending:
        for key in list(pending):
            try:
                next(pending[key])
            except StopIteration as done:
                results[key] = done.value
                del pending[key]
    return tuple(results[k] for k in range(len(stage_generators)))


def _post_kernel(x_ref, ot_ref, cv_ref, prev_ref, next_ref, gmix_ref, gbr_ref, wo_ref, cw_ref,
                 cb_ref, lng_ref, lnb_ref, sw_ref, gpmix_ref, shift_ref, scale_ref, gmlp_ref,
                 gpre_ref, w1_ref, w2_ref, gpmlp_ref, o_ref,
                 uext_ref, ush_ref, cext_ref, new_ref, old_ref, *, n_tiles, ff_chunk):
    j = pl.program_id(1)
    if n_tiles == 1:
        (mixed,) = _alternate(_mix_tile(
            0, 1, x_ref, ot_ref, cv_ref, prev_ref, next_ref, gmix_ref, gbr_ref, wo_ref, cw_ref,
            cb_ref, lng_ref, lnb_ref, sw_ref, gpmix_ref, uext_ref, ush_ref, cext_ref))
        (o_ref[0],) = _alternate(_mlp_tile(mixed, shift_ref, scale_ref, gmlp_ref, gpre_ref,
                                           w1_ref, w2_ref, gpmlp_ref, ff_chunk))
        return

    @pl.when(j == 0)
    def _():
        old_ref[...] = jnp.zeros_like(old_ref)

    mix = _mix_tile(jnp.minimum(j, n_tiles - 1), n_tiles, x_ref, ot_ref, cv_ref, prev_ref,
                    next_ref, gmix_ref, gbr_ref, wo_ref, cw_ref, cb_ref, lng_ref, lnb_ref, sw_ref,
                    gpmix_ref, uext_ref, ush_ref, cext_ref)
    mlp = _mlp_tile(old_ref[...], shift_ref, scale_ref, gmlp_ref, gpre_ref, w1_ref, w2_ref,
                    gpmlp_ref, ff_chunk)
    mixed, out = _alternate(mix, mlp)
    new_ref[...] = mixed
    o_ref[0] = out
    old_ref[...] = new_ref[...]


def _post(x, ot, cv, row0, mods, wts, tm, ff_chunk=1024):
    bsz, seq, _ = x.shape
    n_t = seq // tm
    hb = tm // HALO
    t0, h0 = row0 // tm, row0 // HALO
    n_h = cv.shape[1] // HALO
    cw = cv.shape[-1]
    lag = 1 if n_t > 1 else 0
    tile = lambda i: jnp.minimum(i, n_t - 1)
    vec = lambda b, i: (b, 0, 0)
    single = lambda shape: pl.BlockSpec(shape, lambda b, i: (0,) * len(shape),
                                        pipeline_mode=pl.Buffered(1))
    return pl.pallas_call(
        functools.partial(_post_kernel, n_tiles=n_t, ff_chunk=ff_chunk),
        grid=(bsz, n_t + lag),
        in_specs=[
            pl.BlockSpec((1, tm, D_MODEL), lambda b, i: (b, tile(i), 0)),
            pl.BlockSpec((1, MLA_WIDTH, tm), lambda b, i: (b, 0, tile(i))),
            pl.BlockSpec((1, tm, cw), lambda b, i: (b, tile(i) + t0, 0)),
            pl.BlockSpec((1, HALO, cw),
                         lambda b, i: (b, jnp.maximum(h0 + tile(i) * hb - 1, 0), 0)),
            pl.BlockSpec((1, HALO, cw),
                         lambda b, i: (b, jnp.minimum(h0 + (tile(i) + 1) * hb, n_h - 1), 0)),
            pl.BlockSpec((1, 1, D_MODEL), vec),
            _const_spec((1, D_MODEL)),
            single((D_MODEL, D_MODEL)),
            _const_spec((CONF_K, CONF_WIDTH)),
            _const_spec((1, CONF_WIDTH)),
            _const_spec((1, CONF_WIDTH)),
            _const_spec((1, CONF_WIDTH)),
            _const_spec((SC_K, SC_WIDTH)),
            _const_spec((1, D_MODEL)),
            pl.BlockSpec((1, 1, D_MODEL), vec),
            pl.BlockSpec((1, 1, D_MODEL), vec),
            pl.BlockSpec((1, 1, D_MODEL), vec),
            _const_spec((1, D_MODEL)),
            single((D_MODEL, D_FF)),
            single((D_FF, D_MODEL)),
            _const_spec((1, D_MODEL)),
        ],
        out_specs=pl.BlockSpec((1, tm, D_MODEL), lambda b, i: (b, jnp.maximum(i - lag, 0), 0)),
        out_shape=jax.ShapeDtypeStruct((bsz, seq, D_MODEL), F32),
        scratch_shapes=[
            pltpu.VMEM((tm + 2 * HALO, CONF_WIDTH), F32),
            pltpu.VMEM((SUBLANES, tm + SUBLANES * (pl.cdiv(CONF_K, SUBLANES) - 1), CONF_WIDTH), F32),
            pltpu.VMEM((tm + 2 * SUBLANES, SC_WIDTH), F32),
            pltpu.VMEM((tm, D_MODEL), F32),
            pltpu.VMEM((tm, D_MODEL), F32),
        ],
        compiler_params=pltpu.CompilerParams(
            dimension_semantics=("arbitrary", "arbitrary"), vmem_limit_bytes=VMEM_LIMIT),
        name="post",
    )(x, ot, cv, cv, cv, mods[0], wts["g_branch"], wts["w_o"], wts["conf_dw_w"], wts["conf_dw_b"],
      wts["conf_ln_g"], wts["conf_ln_b"], wts["sc_dw_w"], wts["g_post_mix"], mods[1], mods[2],
      mods[3], wts["g_pre_mlp"], wts["w_mlp_in"], wts["w_mlp_out"], wts["g_post_mlp"])


def _rotate_half_cols(w):
    hh = AXIS_DIM // 2
    parts = []
    for ax in range(2):
        blk = w[..., ax * AXIS_DIM:(ax + 1) * AXIS_DIM]
        parts.append(jnp.concatenate([-blk[..., hh:], blk[..., :hh]], axis=-1))
    return jnp.concatenate(parts, axis=-1)


def _head_group(nope, rope):
    pad = jnp.zeros(nope.shape[:-1] + (HEAD_PAD - QK_DIM,), nope.dtype)
    return jnp.concatenate([nope, rope, pad], axis=-1)


def _pack_layer(i, w_in, g_q, w_q_b, g_kv, w_kv_b, p):
    win = w_in[i]
    zeros_nope = jnp.zeros((D_MODEL, QK_NOPE), F32)
    w_kr = win[:, Q_LORA + KV_LORA:Q_LORA + KV_LORA + QK_ROPE]
    conf0 = Q_LORA + KV_LORA + QK_ROPE
    sc0 = conf0 + 2 * CONF_WIDTH
    win_packed = jnp.concatenate([
        win[:, 0:Q_LORA],
        win[:, Q_LORA:Q_LORA + KV_LORA],
        _head_group(zeros_nope, w_kr),
        _head_group(zeros_nope, _rotate_half_cols(w_kr)),
        win[:, conf0:sc0],
        win[:, sc0:],
    ], axis=-1)

    wq = w_q_b[i].reshape(Q_LORA, MLA_HEADS, QK_DIM)
    wq_nope, wq_rope = wq[..., :QK_NOPE], wq[..., QK_NOPE:]
    wq_main = _head_group(wq_nope, wq_rope).reshape(Q_LORA, MLA_HEADS * HEAD_PAD)
    wq_rot = _head_group(jnp.zeros_like(wq_nope), _rotate_half_cols(wq_rope)).reshape(
        Q_LORA, MLA_HEADS * HEAD_PAD)

    wkv = w_kv_b[i].reshape(KV_LORA, MLA_HEADS, QK_NOPE + V_DIM)
    wk = _head_group(wkv[..., :QK_NOPE], jnp.zeros((KV_LORA, MLA_HEADS, QK_ROPE), F32)).reshape(
        KV_LORA, MLA_HEADS * HEAD_PAD)
    wv_t = jnp.transpose(wkv[..., QK_NOPE:], (1, 2, 0))

    row = lambda v: v[i][None, :]
    return {
        "g_pre_mix": row(p["g_pre_mix"]),
        "w_in": win_packed.astype(BF16),
        "g_q": row(g_q) * (SM_SCALE * math.log2(math.e)),
        "w_q_t": jnp.concatenate([wq_main, wq_rot], axis=-1).T.astype(BF16),
        "g_kv": row(g_kv),
        "w_k": wk.astype(BF16),
        "w_vt": wv_t.reshape(MLA_WIDTH, KV_LORA).astype(BF16),
        "g_branch": row(p["g_branch"]),
        "w_o": p["w_o"][i].astype(BF16),
        "conf_dw_w": p["conf_dw_w"][i],
        "conf_dw_b": row(p["conf_dw_b"]),
        "conf_ln_g": row(p["conf_ln_g"]),
        "conf_ln_b": row(p["conf_ln_b"]),
        "sc_dw_w": p["sc_dw_w"][i],
        "g_post_mix": row(p["g_post_mix"]),
        "g_pre_mlp": row(p["g_pre_mlp"]),
        "w_mlp_in": p["w_mlp_in"][i].astype(BF16),
        "w_mlp_out": p["w_mlp_out"][i].astype(BF16),
        "g_post_mlp": row(p["g_post_mlp"]),
    }


def _rope_tables(n_lat):
    rows = n_lat // GRID_W
    row = jnp.broadcast_to(jnp.arange(rows)[:, None], (rows, GRID_W)).reshape(-1).astype(F32)
    col = jnp.broadcast_to(jnp.arange(GRID_W)[None, :], (rows, GRID_W)).reshape(-1).astype(F32)
    inv = 1.0 / (ROPE_THETA ** (jnp.arange(0, AXIS_DIM, 2, dtype=F32) / AXIS_DIM))
    ar = row[:, None] * inv
    ac = col[:, None] * inv
    ang = jnp.concatenate([ar, ar, ac, ac], axis=-1)
    return (_head_group(jnp.ones((n_lat, QK_NOPE), F32), jnp.cos(ang)),
            _head_group(jnp.zeros((n_lat, QK_NOPE), F32), jnp.sin(ang)))


def _identity_tables(n):
    return (_head_group(jnp.ones((n, QK_NOPE), F32), jnp.ones((n, QK_ROPE), F32)),
            jnp.zeros((n, HEAD_PAD), F32))


def kernel(x, c, ctx, c_ctx, w_mod, b_mod, g_pre_mix, g_post_mix, g_pre_mlp, g_post_mlp, w_in,
           g_q, w_q_b, g_kv, w_kv_b, conf_dw_w, conf_dw_b, conf_ln_g, conf_ln_b, sc_dw_w,
           g_branch, w_o, w_mlp_in, w_mlp_out):
    bsz, n_lat, _ = x.shape
    n_ctx = ctx.shape[1]
    depth = w_mod.shape[0]
    params = dict(g_pre_mix=g_pre_mix, g_post_mix=g_post_mix, g_pre_mlp=g_pre_mlp,
                  g_post_mlp=g_post_mlp, conf_dw_w=conf_dw_w, conf_dw_b=conf_dw_b,
                  conf_ln_g=conf_ln_g, conf_ln_b=conf_ln_b, sc_dw_w=sc_dw_w, g_branch=g_branch,
                  w_o=w_o, w_mlp_in=w_mlp_in, w_mlp_out=w_mlp_out)

    mod_rows = SUBLANES * pl.cdiv(bsz + 1, SUBLANES)
    cvecs = jnp.concatenate(
        [c, c_ctx[None, :], jnp.zeros((mod_rows - bsz - 1, D_MODEL), F32)], axis=0)
    mod = _modulation(cvecs, w_mod, b_mod).reshape(depth, mod_rows, N_MOD, D_MODEL)

    cos_l, sin_l = _rope_tables(n_lat)
    cos_c, sin_c = _identity_tables(n_ctx)
    cos_t = jnp.concatenate([cos_l, cos_c], axis=0)
    sin_t = jnp.concatenate([sin_l, sin_c], axis=0)
    n_all = n_lat + n_ctx
    tm = min(256, n_ctx)
    tq_lat, tk_lat = 512, 768
    assert n_lat % tm == 0 and n_ctx % tm == 0 and n_lat % n_ctx == 0
    tm_post = 256
    assert n_lat % tq_lat == 0 and n_all % tk_lat == 0 and n_lat % tm_post == 0
    ctx_blk = n_lat // n_ctx

    xl, xc = x, ctx
    for i in range(depth):
        last = i == depth - 1
        wts = _pack_layer(i, w_in, g_q, w_q_b, g_kv, w_kv_b, params)
        ml = [mod[i, :bsz, j][:, None, :] for j in range(N_MOD)]
        mc1 = [mod[i, bsz, j][None, None, :] for j in range(N_MOD)]

        q, k, vt, cv = _in_proj(xl, xc, ml[0:2], mc1[0:2], wts, cos_t, sin_t, tm)

        ot_l = _attention(q, k, vt, n_lat, n_all, 0, tq_lat, tk_lat)
        xl = _post(xl, ot_l, cv, 0, ml[2:6], wts, tm_post)

        if not last:
            mc = [jnp.broadcast_to(v, (bsz, 1, D_MODEL)) for v in mc1]
            ot_c = _attention(q, k, vt, n_ctx, n_ctx, ctx_blk, n_ctx, n_ctx)
            xc = _post(xc, ot_c, cv, n_lat, mc[2:6], wts, tm)
    return xl
```

```python
import functools
import math

import jax
import jax.numpy as jnp
from jax import lax
from jax.experimental import pallas as pl
from jax.experimental.pallas import tpu as pltpu

F32 = jnp.float32
BF16 = jnp.bfloat16

D_MODEL = 1024
GRID_W = 64
N_MOD = 6
MLA_HEADS = 8
QK_NOPE = 64
QK_ROPE = 32
QK_DIM = QK_NOPE + QK_ROPE
V_DIM = 64
MLA_WIDTH = MLA_HEADS * V_DIM
Q_LORA = 256
KV_LORA = 128
AXIS_DIM = QK_ROPE // 2
ROPE_THETA = 10000.0
CONF_WIDTH = 256
CONF_K = 31
SC_WIDTH = 256
SC_K = 3
D_FF = 4 * D_MODEL
EPS = 1e-6
SM_SCALE = 1.0 / math.sqrt(QK_DIM)

LANES = 128
SUBLANES = 8
BF16_ROWS = 16
MXU_DIM = 256

HEAD_PAD = LANES
V_ROWS = V_DIM + BF16_ROWS
HALO = 16

C_Q = 0
C_CKV = C_Q + Q_LORA
C_KR = C_CKV + KV_LORA
C_CONF_A = C_KR + HEAD_PAD
C_CONF_G = C_CONF_A + CONF_WIDTH
C_SC_B = C_CONF_G + CONF_WIDTH
C_SC_C = C_SC_B + SC_WIDTH
C_SC_H = C_SC_C + SC_WIDTH
IN_PACKED = C_SC_H + SC_WIDTH

NEG_BIG = -1e30
PAIR_UNROLL = 8
HEADS_PER_STEP = 2
SCORE_ROW_PAD = 2 * LANES
VMEM_LIMIT = 56 * 1024 * 1024


def _const_spec(shape):
    nd = len(shape)
    return pl.BlockSpec(shape, lambda *_: (0,) * nd)


def _rms_scale(v, width):
    return lax.rsqrt(jnp.sum(v * v, axis=-1, keepdims=True) * (1.0 / width) + EPS)


def _mod_kernel(c_ref, w_ref, b_ref, o_ref):
    cv = c_ref[...]
    s = cv * jax.nn.sigmoid(cv)
    w = w_ref[0]
    s_hi = s.astype(BF16)
    s_lo = (s - s_hi.astype(F32)).astype(BF16)
    w_hi = w.astype(BF16)
    w_lo = (w - w_hi.astype(F32)).astype(BF16)
    acc = jnp.dot(s_hi, w_lo, preferred_element_type=F32)
    acc = acc + jnp.dot(s_lo, w_hi, preferred_element_type=F32)
    acc = acc + jnp.dot(s_hi, w_hi, preferred_element_type=F32)
    o_ref[0] = acc + b_ref[0]


def _modulation(cvecs, w_mod, b_mod):
    depth = w_mod.shape[0]
    rows = cvecs.shape[0]
    return pl.pallas_call(
        _mod_kernel,
        grid=(depth, N_MOD),
        in_specs=[
            _const_spec((rows, D_MODEL)),
            pl.BlockSpec((1, D_MODEL, D_MODEL), lambda l, j: (l, 0, j)),
            pl.BlockSpec((1, 1, D_MODEL), lambda l, j: (l, 0, j)),
        ],
        out_specs=pl.BlockSpec((1, rows, D_MODEL), lambda l, j: (l, 0, j)),
        out_shape=jax.ShapeDtypeStruct((depth, rows, N_MOD * D_MODEL), F32),
        compiler_params=pltpu.CompilerParams(
            dimension_semantics=("arbitrary", "arbitrary"), vmem_limit_bytes=VMEM_LIMIT),
        name="modulation",
    )(cvecs, w_mod, b_mod.reshape(depth, 1, N_MOD * D_MODEL))


def _in_proj_kernel(xl_ref, xc_ref, shl_ref, scl_ref, shc_ref, scc_ref, gpre_ref, win_ref, gq_ref,
                    wqt_ref, gkv_ref, wk_ref, wvt_ref, cos_ref, sin_ref, cost_ref, sint_ref,
                    q_ref, k_ref, vt_ref, cv_ref,
                    *, n_lat_tiles):
    is_ctx = pl.program_id(1) >= n_lat_tiles
    x = jnp.where(is_ctx, xc_ref[0], xl_ref[0])
    shift = jnp.where(is_ctx, shc_ref[0], shl_ref[0])
    scale = jnp.where(is_ctx, scc_ref[0], scl_ref[0])
    gain = gpre_ref[...] * (1.0 + scale)
    h = (x * _rms_scale(x, D_MODEL)) * gain + shift
    z = jnp.dot(h.astype(BF16), win_ref[...], preferred_element_type=F32)
    cos = cos_ref[...]
    sin = sin_ref[...]

    zq = z[:, C_Q:C_Q + Q_LORA]
    zqn = (zq * _rms_scale(zq, Q_LORA)) * gq_ref[...]
    qq = lax.dot_general(wqt_ref[...], zqn.astype(BF16), (((1,), (1,)), ((), ())),
                         preferred_element_type=F32)
    cos_q = cost_ref[...]
    sin_q = sint_ref[...]
    half = MLA_HEADS * HEAD_PAD
    for hd in range(MLA_HEADS):
        lo = hd * HEAD_PAD
        qh = qq[lo:lo + HEAD_PAD] * cos_q + qq[half + lo:half + lo + HEAD_PAD] * sin_q
        q_ref[0, lo:lo + HEAD_PAD, :] = qh.astype(BF16)

    ckv = z[:, C_CKV:C_CKV + KV_LORA]
    ckvn = ((ckv * _rms_scale(ckv, KV_LORA)) * gkv_ref[...]).astype(BF16)
    kk = jnp.dot(ckvn, wk_ref[...], preferred_element_type=F32)
    krp = z[:, C_KR:C_KR + HEAD_PAD]
    kr = krp * cos + pltpu.roll(krp, HEAD_PAD - QK_ROPE, axis=1) * sin
    for hd in range(MLA_HEADS):
        lo = hd * HEAD_PAD
        k_ref[0, :, lo:lo + HEAD_PAD] = (kk[:, lo:lo + HEAD_PAD] + kr).astype(BF16)

    vt = lax.dot_general(wvt_ref[...], ckvn, (((1,), (1,)), ((), ())),
                         preferred_element_type=F32)
    ones = jnp.ones((V_ROWS - V_DIM, vt.shape[1]), BF16)
    for hd in range(MLA_HEADS):
        vt_ref[0, hd * V_ROWS:hd * V_ROWS + V_DIM, :] = (
            vt[hd * V_DIM:(hd + 1) * V_DIM].astype(BF16))
        vt_ref[0, hd * V_ROWS + V_DIM:(hd + 1) * V_ROWS, :] = ones

    a = z[:, C_CONF_A:C_CONF_A + CONF_WIDTH]
    g = z[:, C_CONF_G:C_CONF_G + CONF_WIDTH]
    cv_ref[0, :, 0:CONF_WIDTH] = a * jax.nn.sigmoid(g)
    cv_ref[0, :, CONF_WIDTH:CONF_WIDTH + SC_WIDTH] = (
        z[:, C_SC_C:C_SC_C + SC_WIDTH] * z[:, C_SC_H:C_SC_H + SC_WIDTH])
    cv_ref[0, :, CONF_WIDTH + SC_WIDTH:] = z[:, C_SC_B:C_SC_B + SC_WIDTH]


def _in_proj(xl, xc, mod_l, mod_c, wts, cos_t, sin_t, tm):
    bsz, n_lat, _ = xl.shape
    n_ctx = xc.shape[1]
    n_lt, n_ct = n_lat // tm, n_ctx // tm
    n_all = n_lat + n_ctx
    tok = lambda b, i: (b, i, 0)
    vec = lambda b, i: (b, 0, 0)
    cw = CONF_WIDTH + 2 * SC_WIDTH
    return pl.pallas_call(
        functools.partial(_in_proj_kernel, n_lat_tiles=n_lt),
        grid=(bsz, n_lt + n_ct),
        in_specs=[
            pl.BlockSpec((1, tm, D_MODEL), lambda b, i: (b, jnp.minimum(i, n_lt - 1), 0)),
            pl.BlockSpec((1, tm, D_MODEL), lambda b, i: (b, jnp.maximum(i - n_lt, 0), 0)),
            pl.BlockSpec((1, 1, D_MODEL), vec),
            pl.BlockSpec((1, 1, D_MODEL), vec),
            _const_spec((1, 1, D_MODEL)),
            _const_spec((1, 1, D_MODEL)),
            _const_spec((1, D_MODEL)),
            _const_spec((D_MODEL, IN_PACKED)),
            _const_spec((1, Q_LORA)),
            _const_spec((2 * MLA_HEADS * HEAD_PAD, Q_LORA)),
            _const_spec((1, KV_LORA)),
            _const_spec((KV_LORA, MLA_HEADS * HEAD_PAD)),
            _const_spec((MLA_WIDTH, KV_LORA)),
            pl.BlockSpec((tm, HEAD_PAD), lambda b, i: (i, 0)),
            pl.BlockSpec((tm, HEAD_PAD), lambda b, i: (i, 0)),
            pl.BlockSpec((HEAD_PAD, tm), lambda b, i: (0, i)),
            pl.BlockSpec((HEAD_PAD, tm), lambda b, i: (0, i)),
        ],
        out_specs=[
            pl.BlockSpec((1, MLA_HEADS * HEAD_PAD, tm), lambda b, i: (b, 0, i)),
            pl.BlockSpec((1, tm, MLA_HEADS * HEAD_PAD), tok),
            pl.BlockSpec((1, MLA_HEADS * V_ROWS, tm), lambda b, i: (b, 0, i)),
            pl.BlockSpec((1, tm, cw), tok),
        ],
        out_shape=[
            jax.ShapeDtypeStruct((bsz, MLA_HEADS * HEAD_PAD, n_all), BF16),
            jax.ShapeDtypeStruct((bsz, n_all, MLA_HEADS * HEAD_PAD), BF16),
            jax.ShapeDtypeStruct((bsz, MLA_HEADS * V_ROWS, n_all), BF16),
            jax.ShapeDtypeStruct((bsz, n_all, cw), F32),
        ],
        compiler_params=pltpu.CompilerParams(
            dimension_semantics=("arbitrary", "arbitrary"), vmem_limit_bytes=VMEM_LIMIT),
        name="in_proj",
    )(xl, xc, mod_l[0], mod_l[1], mod_c[0], mod_c[1], wts["g_pre_mix"], wts["w_in"], wts["g_q"],
      wts["w_q_t"], wts["g_kv"], wts["w_k"], wts["w_vt"], cos_t, sin_t, cos_t.T, sin_t.T)


def _attn_kernel(qt_ref, k_ref, vt_ref, o_ref, sa_ref, sb_ref, acc_ref, *, tq, tk):
    n_h = qt_ref.shape[1] // HEAD_PAD
    n_q = qt_ref.shape[2] // tq
    n_c = k_ref.shape[1] // tk
    total = n_h * n_q * n_c
    acc_ref[...] = jnp.zeros_like(acc_ref)

    def advance(pos):
        hd, qb, ck = pos
        wrap_c = ck == n_c - 1
        wrap_q = jnp.logical_and(wrap_c, qb == n_q - 1)
        return (jnp.where(wrap_q, hd + 1, hd),
                jnp.where(wrap_q, 0, jnp.where(wrap_c, qb + 1, qb)),
                jnp.where(wrap_c, 0, ck + 1))

    def scores(pos, s_ref):
        hd, qb, ck = pos
        h0 = pl.multiple_of(hd * HEAD_PAD, HEAD_PAD)
        qt = qt_ref[0, pl.ds(h0, HEAD_PAD), pl.ds(pl.multiple_of(qb * tq, tq), tq)]
        kc = k_ref[0, pl.ds(pl.multiple_of(ck * tk, tk), tk), pl.ds(h0, HEAD_PAD)]
        s = jnp.dot(kc, qt, preferred_element_type=F32)
        s_ref[:, 0:tq] = s
        return jnp.max(s, axis=0, keepdims=True)

    def accumulate(pos, s_ref, s_max, m):
        hd, qb, ck = pos
        m_prev = jnp.where(ck == 0, NEG_BIG, m)
        m_new = jnp.maximum(m_prev, s_max)
        alpha = jnp.exp2(m_prev - m_new)
        p = jnp.exp2(s_ref[:, 0:tq] - m_new).astype(BF16)
        vc = vt_ref[0, pl.ds(pl.multiple_of(hd * V_ROWS, BF16_ROWS), V_ROWS),
                    pl.ds(pl.multiple_of(ck * tk, tk), tk)]
        acc = acc_ref[...] * alpha + jnp.dot(vc, p, preferred_element_type=F32)
        acc_ref[...] = acc
        o_ref[0, pl.ds(pl.multiple_of(hd * V_DIM, SUBLANES), V_DIM),
              pl.ds(pl.multiple_of(qb * tq, tq), tq)] = acc[0:V_DIM] * (1.0 / acc[V_DIM:V_DIM + 1])
        return m_new

    zero = jnp.int32(0)
    pos = (zero, zero, zero)
    m = jnp.full((1, tq), NEG_BIG, F32)
    max_a = scores(pos, sa_ref)

    def body(_, carry):
        h0, q0, c0, max_a, m = carry
        pos0 = (h0, q0, c0)
        pos1 = advance(pos0)
        pos2 = advance(pos1)
        max_b = scores(pos1, sb_ref)
        m = accumulate(pos0, sa_ref, max_a, m)
        max_a = scores(pos2, sa_ref)
        m = accumulate(pos1, sb_ref, max_b, m)
        return pos2 + (max_a, m)

    n_pairs = (total - 1) // 2
    if n_pairs > 0:
        carry = lax.fori_loop(0, n_pairs, body, pos + (max_a, m), unroll=PAIR_UNROLL)
        pos, max_a, m = carry[0:3], carry[3], carry[4]
    if total % 2 == 0:
        pos1 = advance(pos)
        max_b = scores(pos1, sb_ref)
        m = accumulate(pos, sa_ref, max_a, m)
        accumulate(pos1, sb_ref, max_b, m)
    else:
        accumulate(pos, sa_ref, max_a, m)


def _attention(q, k, vt, seq, t, blk, tq, tk):
    bsz = q.shape[0]
    return pl.pallas_call(
        functools.partial(_attn_kernel, tq=tq, tk=tk),
        grid=(bsz, MLA_HEADS // HEADS_PER_STEP),
        in_specs=[
            pl.BlockSpec((1, HEADS_PER_STEP * HEAD_PAD, seq), lambda b, h: (b, h, blk)),
            pl.BlockSpec((1, t, HEADS_PER_STEP * HEAD_PAD), lambda b, h: (b, blk, h)),
            pl.BlockSpec((1, HEADS_PER_STEP * V_ROWS, t), lambda b, h: (b, h, blk)),
        ],
        out_specs=pl.BlockSpec((1, HEADS_PER_STEP * V_DIM, seq), lambda b, h: (b, h, 0)),
        out_shape=jax.ShapeDtypeStruct((bsz, MLA_WIDTH, seq), F32),
        scratch_shapes=[
            pltpu.VMEM((tk, tq + SCORE_ROW_PAD), F32),
            pltpu.VMEM((tk, tq + SCORE_ROW_PAD), F32),
            pltpu.VMEM((V_ROWS, tq), F32),
        ],
        compiler_params=pltpu.CompilerParams(
            dimension_semantics=("arbitrary", "arbitrary"), vmem_limit_bytes=VMEM_LIMIT),
        name="attention",
    )(q, k, vt)


def _mix_tile(tile, n_tiles, x_ref, ot_ref, cv_ref, prev_ref, next_ref, gate_ref, gbr_ref, wo_ref,
              cw_ref, cb_ref, lng_ref, lnb_ref, sw_ref, gpost_ref, uext_ref, ush_ref, cext_ref):
    tm = x_ref.shape[1]
    has_prev = jnp.where(tile > 0, 1.0, 0.0)
    has_next = jnp.where(tile < n_tiles - 1, 1.0, 0.0)
    c0, c1, c2 = 0, CONF_WIDTH, CONF_WIDTH + SC_WIDTH

    a = ot_ref[0].T
    an = (a * _rms_scale(a, MLA_WIDTH)) * gbr_ref[:, 0:MLA_WIDTH]
    y = jnp.dot(an.astype(BF16), wo_ref[0:MLA_WIDTH, :], preferred_element_type=F32)

    yield
    uext_ref[0:HALO, :] = prev_ref[0, :, c0:c1] * has_prev
    uext_ref[HALO:HALO + tm, :] = cv_ref[0, :, c0:c1]
    uext_ref[HALO + tm:, :] = next_ref[0, :, c0:c1] * has_next
    base = HALO - CONF_K // 2
    n_a = ush_ref.shape[1] // SUBLANES - tm // SUBLANES + 1
    conv = cb_ref[...]
    for r in range(SUBLANES):
        if r == SUBLANES // 2:
            yield
        ush_ref[r] = uext_ref[base + r:base + r + ush_ref.shape[1], :]
        for a_i in range(n_a):
            t = SUBLANES * a_i + r
            if t < CONF_K:
                conv = conv + cw_ref[t:t + 1, :] * ush_ref[r, SUBLANES * a_i:SUBLANES * a_i + tm, :]
    yield
    mu = jnp.sum(conv, axis=-1, keepdims=True) * (1.0 / CONF_WIDTH)
    d = conv - mu
    ln = (d * _rms_scale(d, CONF_WIDTH)) * lng_ref[...] + lnb_ref[...]
    cf = ln * jax.nn.sigmoid(ln)
    cfn = (cf * _rms_scale(cf, CONF_WIDTH)) * gbr_ref[:, MLA_WIDTH:MLA_WIDTH + CONF_WIDTH]
    y = y + jnp.dot(cfn.astype(BF16), wo_ref[MLA_WIDTH:MLA_WIDTH + CONF_WIDTH, :],
                    preferred_element_type=F32)

    yield
    cext_ref[0:SUBLANES, :] = prev_ref[0, HALO - SUBLANES:, c1:c2] * has_prev
    cext_ref[SUBLANES:SUBLANES + tm, :] = cv_ref[0, :, c1:c2]
    cext_ref[SUBLANES + tm:, :] = next_ref[0, 0:SUBLANES, c1:c2] * has_next
    base = SUBLANES - SC_K // 2
    sc = sw_ref[0:1, :] * cext_ref[base:base + tm, :]
    for t in range(1, SC_K):
        sc = sc + sw_ref[t:t + 1, :] * cext_ref[base + t:base + t + tm, :]
    sc = cv_ref[0, :, c2:] * sc
    scn = (sc * _rms_scale(sc, SC_WIDTH)) * gbr_ref[:, MLA_WIDTH + CONF_WIDTH:]
    y = y + jnp.dot(scn.astype(BF16), wo_ref[MLA_WIDTH + CONF_WIDTH:, :],
                    preferred_element_type=F32)

    yn = (y * _rms_scale(y, D_MODEL)) * gpost_ref[...]
    return x_ref[0] + gate_ref[0] * yn


def _mlp_tile(x, shift_ref, scale_ref, gate_ref, gpre_ref, w1_ref, w2_ref, gpost_ref, ff_chunk):
    gain = gpre_ref[...] * (1.0 + scale_ref[0])
    h = ((x * _rms_scale(x, D_MODEL)) * gain + shift_ref[0]).astype(BF16)
    y = None
    for j in range(D_FF // ff_chunk):
        yield
        lo = j * ff_chunk
        a = jnp.maximum(jnp.dot(h, w1_ref[:, lo:lo + ff_chunk], preferred_element_type=F32), 0.0)
        part = jnp.dot((a * a).astype(BF16), w2_ref[lo:lo + ff_chunk, :],
                       preferred_element_type=F32)
        y = part if y is None else y + part
    yn = (y * _rms_scale(y, D_MODEL)) * gpost_ref[...]
    return x + gate_ref[0] * yn


def _alternate(*stage_generators):
    pending = dict(enumerate(stage_generators))
    results = {}
    while pending:
        for key in list(pending):
            try:
                next(pending[key])
            except StopIteration as done:
                results[key] = done.value
                del pending[key]
    return tuple(results[k] for k in range(len(stage_generators)))


def _post_kernel(x_ref, ot_ref, cv_ref, prev_ref, next_ref, gmix_ref, gbr_ref, wo_ref, cw_ref,
                 cb_ref, lng_ref, lnb_ref, sw_ref, gpmix_ref, shift_ref, scale_ref, gmlp_ref,
                 gpre_ref, w1_ref, w2_ref, gpmlp_ref, o_ref,
                 uext_ref, ush_ref, cext_ref, new_ref, old_ref, *, n_tiles, ff_chunk):
    j = pl.program_id(1)
    if n_tiles == 1:
        (mixed,) = _alternate(_mix_tile(
            0, 1, x_ref, ot_ref, cv_ref, prev_ref, next_ref, gmix_ref, gbr_ref, wo_ref, cw_ref,
            cb_ref, lng_ref, lnb_ref, sw_ref, gpmix_ref, uext_ref, ush_ref, cext_ref))
        (o_ref[0],) = _alternate(_mlp_tile(mixed, shift_ref, scale_ref, gmlp_ref, gpre_ref,
                                           w1_ref, w2_ref, gpmlp_ref, ff_chunk))
        return

    @pl.when(j == 0)
    def _():
        old_ref[...] = jnp.zeros_like(old_ref)

    mix = _mix_tile(jnp.minimum(j, n_tiles - 1), n_tiles, x_ref, ot_ref, cv_ref, prev_ref,
                    next_ref, gmix_ref, gbr_ref, wo_ref, cw_ref, cb_ref, lng_ref, lnb_ref, sw_ref,
                    gpmix_ref, uext_ref, ush_ref, cext_ref)
    mlp = _mlp_tile(old_ref[...], shift_ref, scale_ref, gmlp_ref, gpre_ref, w1_ref, w2_ref,
                    gpmlp_ref, ff_chunk)
    mixed, out = _alternate(mix, mlp)
    new_ref[...] = mixed
    o_ref[0] = out
    old_ref[...] = new_ref[...]


def _post(x, ot, cv, row0, mods, wts, tm, ff_chunk=1024):
    bsz, seq, _ = x.shape
    n_t = seq // tm
    hb = tm // HALO
    t0, h0 = row0 // tm, row0 // HALO
    n_h = cv.shape[1] // HALO
    cw = cv.shape[-1]
    lag = 1 if n_t > 1 else 0
    tile = lambda i: jnp.minimum(i, n_t - 1)
    vec = lambda b, i: (b, 0, 0)
    single = lambda shape: pl.BlockSpec(shape, lambda b, i: (0,) * len(shape),
                                        pipeline_mode=pl.Buffered(1))
    return pl.pallas_call(
        functools.partial(_post_kernel, n_tiles=n_t, ff_chunk=ff_chunk),
        grid=(bsz, n_t + lag),
        in_specs=[
            pl.BlockSpec((1, tm, D_MODEL), lambda b, i: (b, tile(i), 0)),
            pl.BlockSpec((1, MLA_WIDTH, tm), lambda b, i: (b, 0, tile(i))),
            pl.BlockSpec((1, tm, cw), lambda b, i: (b, tile(i) + t0, 0)),
            pl.BlockSpec((1, HALO, cw),
                         lambda b, i: (b, jnp.maximum(h0 + tile(i) * hb - 1, 0), 0)),
            pl.BlockSpec((1, HALO, cw),
                         lambda b, i: (b, jnp.minimum(h0 + (tile(i) + 1) * hb, n_h - 1), 0)),
            pl.BlockSpec((1, 1, D_MODEL), vec),
            _const_spec((1, D_MODEL)),
            single((D_MODEL, D_MODEL)),
            _const_spec((CONF_K, CONF_WIDTH)),
            _const_spec((1, CONF_WIDTH)),
            _const_spec((1, CONF_WIDTH)),
            _const_spec((1, CONF_WIDTH)),
            _const_spec((SC_K, SC_WIDTH)),
            _const_spec((1, D_MODEL)),
            pl.BlockSpec((1, 1, D_MODEL), vec),
            pl.BlockSpec((1, 1, D_MODEL), vec),
            pl.BlockSpec((1, 1, D_MODEL), vec),
            _const_spec((1, D_MODEL)),
            single((D_MODEL, D_FF)),
            single((D_FF, D_MODEL)),
            _const_spec((1, D_MODEL)),
        ],
        out_specs=pl.BlockSpec((1, tm, D_MODEL), lambda b, i: (b, jnp.maximum(i - lag, 0), 0)),
        out_shape=jax.ShapeDtypeStruct((bsz, seq, D_MODEL), F32),
        scratch_shapes=[
            pltpu.VMEM((tm + 2 * HALO, CONF_WIDTH), F32),
            pltpu.VMEM((SUBLANES, tm + SUBLANES * (pl.cdiv(CONF_K, SUBLANES) - 1), CONF_WIDTH), F32),
            pltpu.VMEM((tm + 2 * SUBLANES, SC_WIDTH), F32),
            pltpu.VMEM((tm, D_MODEL), F32),
            pltpu.VMEM((tm, D_MODEL), F32),
        ],
        compiler_params=pltpu.CompilerParams(
            dimension_semantics=("arbitrary", "arbitrary"), vmem_limit_bytes=VMEM_LIMIT),
        name="post",
    )(x, ot, cv, cv, cv, mods[0], wts["g_branch"], wts["w_o"], wts["conf_dw_w"], wts["conf_dw_b"],
      wts["conf_ln_g"], wts["conf_ln_b"], wts["sc_dw_w"], wts["g_post_mix"], mods[1], mods[2],
      mods[3], wts["g_pre_mlp"], wts["w_mlp_in"], wts["w_mlp_out"], wts["g_post_mlp"])


def _rotate_half_cols(w):
    hh = AXIS_DIM // 2
    parts = []
    for ax in range(2):
        blk = w[..., ax * AXIS_DIM:(ax + 1) * AXIS_DIM]
        parts.append(jnp.concatenate([-blk[..., hh:], blk[..., :hh]], axis=-1))
    return jnp.concatenate(parts, axis=-1)


def _head_group(nope, rope):
    pad = jnp.zeros(nope.shape[:-1] + (HEAD_PAD - QK_DIM,), nope.dtype)
    return jnp.concatenate([nope, rope, pad], axis=-1)


def _pack_layer(i, w_in, g_q, w_q_b, g_kv, w_kv_b, p):
    win = w_in[i]
    zeros_nope = jnp.zeros((D_MODEL, QK_NOPE), F32)
    w_kr = win[:, Q_LORA + KV_LORA:Q_LORA + KV_LORA + QK_ROPE]
    conf0 = Q_LORA + KV_LORA + QK_ROPE
    sc0 = conf0 + 2 * CONF_WIDTH
    win_packed = jnp.concatenate([
        win[:, 0:Q_LORA],
        win[:, Q_LORA:Q_LORA + KV_LORA],
        jnp.concatenate([zeros_nope, w_kr, _rotate_half_cols(w_kr)], axis=-1),
        win[:, conf0:sc0],
        win[:, sc0:],
    ], axis=-1)

    wq = w_q_b[i].reshape(Q_LORA, MLA_HEADS, QK_DIM)
    wq_nope, wq_rope = wq[..., :QK_NOPE], wq[..., QK_NOPE:]
    wq_main = _head_group(wq_nope, wq_rope).reshape(Q_LORA, MLA_HEADS * HEAD_PAD)
    wq_rot = _head_group(jnp.zeros_like(wq_nope), _rotate_half_cols(wq_rope)).reshape(
        Q_LORA, MLA_HEADS * HEAD_PAD)

    wkv = w_kv_b[i].reshape(KV_LORA, MLA_HEADS, QK_NOPE + V_DIM)
    wk = _head_group(wkv[..., :QK_NOPE], jnp.zeros((KV_LORA, MLA_HEADS, QK_ROPE), F32)).reshape(
        KV_LORA, MLA_HEADS * HEAD_PAD)
    wv_t = jnp.transpose(wkv[..., QK_NOPE:], (1, 2, 0))

    row = lambda v: v[i][None, :]
    return {
        "g_pre_mix": row(p["g_pre_mix"]),
        "w_in": win_packed.astype(BF16),
        "g_q": row(g_q) * (SM_SCALE * math.log2(math.e)),
        "w_q_t": jnp.concatenate([wq_main, wq_rot], axis=-1).T.astype(BF16),
        "g_kv": row(g_kv),
        "w_k": wk.astype(BF16),
        "w_vt": wv_t.reshape(MLA_WIDTH, KV_LORA).astype(BF16),
        "g_branch": row(p["g_branch"]),
        "w_o": p["w_o"][i].astype(BF16),
        "conf_dw_w": p["conf_dw_w"][i],
        "conf_dw_b": row(p["conf_dw_b"]),
        "conf_ln_g": row(p["conf_ln_g"]),
        "conf_ln_b": row(p["conf_ln_b"]),
        "sc_dw_w": p["sc_dw_w"][i],
        "g_post_mix": row(p["g_post_mix"]),
        "g_pre_mlp": row(p["g_pre_mlp"]),
        "w_mlp_in": p["w_mlp_in"][i].astype(BF16),
        "w_mlp_out": p["w_mlp_out"][i].astype(BF16),
        "g_post_mlp": row(p["g_post_mlp"]),
    }


def _rope_tables(n_lat):
    rows = n_lat // GRID_W
    row = jnp.broadcast_to(jnp.arange(rows)[:, None], (rows, GRID_W)).reshape(-1).astype(F32)
    col = jnp.broadcast_to(jnp.arange(GRID_W)[None, :], (rows, GRID_W)).reshape(-1).astype(F32)
    inv = 1.0 / (ROPE_THETA ** (jnp.arange(0, AXIS_DIM, 2, dtype=F32) / AXIS_DIM))
    ar = row[:, None] * inv
    ac = col[:, None] * inv
    ang = jnp.concatenate([ar, ar, ac, ac], axis=-1)
    return (_head_group(jnp.ones((n_lat, QK_NOPE), F32), jnp.cos(ang)),
            _head_group(jnp.zeros((n_lat, QK_NOPE), F32), jnp.sin(ang)))


def _identity_tables(n):
    return (_head_group(jnp.ones((n, QK_NOPE), F32), jnp.ones((n, QK_ROPE), F32)),
            jnp.zeros((n, HEAD_PAD), F32))


def kernel(x, c, ctx, c_ctx, w_mod, b_mod, g_pre_mix, g_post_mix, g_pre_mlp, g_post_mlp, w_in,
           g_q, w_q_b, g_kv, w_kv_b, conf_dw_w, conf_dw_b, conf_ln_g, conf_ln_b, sc_dw_w,
           g_branch, w_o, w_mlp_in, w_mlp_out):
    bsz, n_lat, _ = x.shape
    n_ctx = ctx.shape[1]
    depth = w_mod.shape[0]
    params = dict(g_pre_mix=g_pre_mix, g_post_mix=g_post_mix, g_pre_mlp=g_pre_mlp,
                  g_post_mlp=g_post_mlp, conf_dw_w=conf_dw_w, conf_dw_b=conf_dw_b,
                  conf_ln_g=conf_ln_g, conf_ln_b=conf_ln_b, sc_dw_w=sc_dw_w, g_branch=g_branch,
                  w_o=w_o, w_mlp_in=w_mlp_in, w_mlp_out=w_mlp_out)

    mod_rows = SUBLANES * pl.cdiv(bsz + 1, SUBLANES)
    cvecs = jnp.concatenate(
        [c, c_ctx[None, :], jnp.zeros((mod_rows - bsz - 1, D_MODEL), F32)], axis=0)
    mod = _modulation(cvecs, w_mod, b_mod).reshape(depth, mod_rows, N_MOD, D_MODEL)

    cos_l, sin_l = _rope_tables(n_lat)
    cos_c, sin_c = _identity_tables(n_ctx)
    cos_t = jnp.concatenate([cos_l, cos_c], axis=0)
    sin_t = jnp.concatenate([sin_l, sin_c], axis=0)
    n_all = n_lat + n_ctx
    tm = min(256, n_ctx)
    tq_lat, tk_lat = 512, 768
    assert n_lat % tm == 0 and n_ctx % tm == 0 and n_lat % n_ctx == 0
    tm_post = 256
    assert n_lat % tq_lat == 0 and n_all % tk_lat == 0 and n_lat % tm_post == 0
    ctx_blk = n_lat // n_ctx

    xl, xc = x, ctx
    for i in range(depth):
        last = i == depth - 1
        wts = _pack_layer(i, w_in, g_q, w_q_b, g_kv, w_kv_b, params)
        ml = [mod[i, :bsz, j][:, None, :] for j in range(N_MOD)]
        mc1 = [mod[i, bsz, j][None, None, :] for j in range(N_MOD)]

        q, k, vt, cv = _in_proj(xl, xc, ml[0:2], mc1[0:2], wts, cos_t, sin_t, tm)

        ot_l = _attention(q, k, vt, n_lat, n_all, 0, tq_lat, tk_lat)
        xl = _post(xl, ot_l, cv, 0, ml[2:6], wts, tm_post)

        if not last:
            mc = [jnp.broadcast_to(v, (bsz, 1, D_MODEL)) for v in mc1]
            ot_c = _attention(q, k, vt, n_ctx, n_ctx, ctx_blk, n_ctx, n_ctx)
            xc = _post(xc, ot_c, cv, n_lat, mc[2:6], wts, tm)
    return xl
```

```python
import functools
import math

import jax
import jax.numpy as jnp
from jax import lax
from jax.experimental import pallas as pl
from jax.experimental.pallas import tpu as pltpu

F32 = jnp.float32
BF16 = jnp.bfloat16

D_MODEL = 1024
GRID_W = 64
N_MOD = 6
MLA_HEADS = 8
QK_NOPE = 64
QK_ROPE = 32
QK_DIM = QK_NOPE + QK_ROPE
V_DIM = 64
MLA_WIDTH = MLA_HEADS * V_DIM
Q_LORA = 256
KV_LORA = 128
AXIS_DIM = QK_ROPE // 2
ROPE_THETA = 10000.0
CONF_WIDTH = 256
CONF_K = 31
SC_WIDTH = 256
SC_K = 3
D_FF = 4 * D_MODEL
EPS = 1e-6
SM_SCALE = 1.0 / math.sqrt(QK_DIM)

LANES = 128
SUBLANES = 8
BF16_ROWS = 16
MXU_DIM = 256

HEAD_PAD = LANES
V_ROWS = V_DIM + BF16_ROWS
HALO = 16

C_Q = 0
C_CKV = C_Q + Q_LORA
C_KR = C_CKV + KV_LORA
C_CONF_A = C_KR + HEAD_PAD
C_CONF_G = C_CONF_A + CONF_WIDTH
C_SC_B = C_CONF_G + CONF_WIDTH
C_SC_C = C_SC_B + SC_WIDTH
C_SC_H = C_SC_C + SC_WIDTH
IN_PACKED = C_SC_H + SC_WIDTH

NEG_BIG = -1e30
PAIR_UNROLL = 8
HEADS_PER_STEP = 2
SCORE_ROW_PAD = 2 * LANES
VMEM_LIMIT = 56 * 1024 * 1024


def _const_spec(shape):
    nd = len(shape)
    return pl.BlockSpec(shape, lambda *_: (0,) * nd)


def _rms_scale(v, width):
    return lax.rsqrt(jnp.sum(v * v, axis=-1, keepdims=True) * (1.0 / width) + EPS)


def _mod_kernel(c_ref, w_ref, b_ref, o_ref):
    cv = c_ref[...]
    s = cv * jax.nn.sigmoid(cv)
    w = w_ref[0]
    s_hi = s.astype(BF16)
    s_lo = (s - s_hi.astype(F32)).astype(BF16)
    w_hi = w.astype(BF16)
    w_lo = (w - w_hi.astype(F32)).astype(BF16)
    acc = jnp.dot(s_hi, w_lo, preferred_element_type=F32)
    acc = acc + jnp.dot(s_lo, w_hi, preferred_element_type=F32)
    acc = acc + jnp.dot(s_hi, w_hi, preferred_element_type=F32)
    o_ref[0] = acc + b_ref[0]


def _modulation(cvecs, w_mod, b_mod):
    depth = w_mod.shape[0]
    rows = cvecs.shape[0]
    return pl.pallas_call(
        _mod_kernel,
        grid=(depth, N_MOD),
        in_specs=[
            _const_spec((rows, D_MODEL)),
            pl.BlockSpec((1, D_MODEL, D_MODEL), lambda l, j: (l, 0, j)),
            pl.BlockSpec((1, 1, D_MODEL), lambda l, j: (l, 0, j)),
        ],
        out_specs=pl.BlockSpec((1, rows, D_MODEL), lambda l, j: (l, 0, j)),
        out_shape=jax.ShapeDtypeStruct((depth, rows, N_MOD * D_MODEL), F32),
        compiler_params=pltpu.CompilerParams(
            dimension_semantics=("arbitrary", "arbitrary"), vmem_limit_bytes=VMEM_LIMIT),
        name="modulation",
    )(cvecs, w_mod, b_mod.reshape(depth, 1, N_MOD * D_MODEL))


def _in_proj_kernel(xl_ref, xc_ref, shl_ref, scl_ref, shc_ref, scc_ref, gpre_ref, win_ref, gq_ref,
                    wqt_ref, gkv_ref, wk_ref, wvt_ref, cos_ref, sin_ref, cost_ref, sint_ref,
                    q_ref, k_ref, vt_ref, cv_ref,
                    *, n_lat_tiles):
    is_ctx = pl.program_id(1) >= n_lat_tiles
    x = jnp.where(is_ctx, xc_ref[0], xl_ref[0])
    shift = jnp.where(is_ctx, shc_ref[0], shl_ref[0])
    scale = jnp.where(is_ctx, scc_ref[0], scl_ref[0])
    gain = gpre_ref[...] * (1.0 + scale)
    h = (x * _rms_scale(x, D_MODEL)) * gain + shift
    z = jnp.dot(h.astype(BF16), win_ref[...], preferred_element_type=F32)
    cos = cos_ref[...]
    sin = sin_ref[...]

    zq = z[:, C_Q:C_Q + Q_LORA]
    zqn = (zq * _rms_scale(zq, Q_LORA)) * gq_ref[...]
    qq = lax.dot_general(wqt_ref[...], zqn.astype(BF16), (((1,), (1,)), ((), ())),
                         preferred_element_type=F32)
    cos_r = cost_ref[QK_NOPE:QK_DIM, :]
    sin_r = sint_ref[QK_NOPE:QK_DIM, :]
    for hd in range(MLA_HEADS):
        lo = hd * HEAD_PAD
        rope = qq[lo + QK_NOPE:lo + QK_DIM] * cos_r + qq[lo + QK_DIM:lo + HEAD_PAD] * sin_r
        q_ref[0, lo:lo + QK_NOPE, :] = qq[lo:lo + QK_NOPE].astype(BF16)
        q_ref[0, lo + QK_NOPE:lo + QK_DIM, :] = rope.astype(BF16)
        q_ref[0, lo + QK_DIM:lo + HEAD_PAD, :] = jnp.zeros((HEAD_PAD - QK_DIM, rope.shape[1]), BF16)

    ckv = z[:, C_CKV:C_CKV + KV_LORA]
    ckvn = ((ckv * _rms_scale(ckv, KV_LORA)) * gkv_ref[...]).astype(BF16)
    kk = jnp.dot(ckvn, wk_ref[...], preferred_element_type=F32)
    krp = z[:, C_KR:C_KR + HEAD_PAD]
    kr = krp * cos + pltpu.roll(krp, HEAD_PAD - QK_ROPE, axis=1) * sin
    for hd in range(MLA_HEADS):
        lo = hd * HEAD_PAD
        k_ref[0, :, lo:lo + HEAD_PAD] = (kk[:, lo:lo + HEAD_PAD] + kr).astype(BF16)

    vt = lax.dot_general(wvt_ref[...], ckvn, (((1,), (1,)), ((), ())),
                         preferred_element_type=F32)
    ones = jnp.ones((V_ROWS - V_DIM, vt.shape[1]), BF16)
    for hd in range(MLA_HEADS):
        vt_ref[0, hd * V_ROWS:hd * V_ROWS + V_DIM, :] = (
            vt[hd * V_DIM:(hd + 1) * V_DIM].astype(BF16))
        vt_ref[0, hd * V_ROWS + V_DIM:(hd + 1) * V_ROWS, :] = ones

    a = z[:, C_CONF_A:C_CONF_A + CONF_WIDTH]
    g = z[:, C_CONF_G:C_CONF_G + CONF_WIDTH]
    cv_ref[0, :, 0:CONF_WIDTH] = a * jax.nn.sigmoid(g)
    cv_ref[0, :, CONF_WIDTH:CONF_WIDTH + SC_WIDTH] = (
        z[:, C_SC_C:C_SC_C + SC_WIDTH] * z[:, C_SC_H:C_SC_H + SC_WIDTH])
    cv_ref[0, :, CONF_WIDTH + SC_WIDTH:] = z[:, C_SC_B:C_SC_B + SC_WIDTH]


def _in_proj(xl, xc, mod_l, mod_c, wts, cos_t, sin_t, tm):
    bsz, n_lat, _ = xl.shape
    n_ctx = xc.shape[1]
    n_lt, n_ct = n_lat // tm, n_ctx // tm
    n_all = n_lat + n_ctx
    tok = lambda b, i: (b, i, 0)
    vec = lambda b, i: (b, 0, 0)
    cw = CONF_WIDTH + 2 * SC_WIDTH
    return pl.pallas_call(
        functools.partial(_in_proj_kernel, n_lat_tiles=n_lt),
        grid=(bsz, n_lt + n_ct),
        in_specs=[
            pl.BlockSpec((1, tm, D_MODEL), lambda b, i: (b, jnp.minimum(i, n_lt - 1), 0)),
            pl.BlockSpec((1, tm, D_MODEL), lambda b, i: (b, jnp.maximum(i - n_lt, 0), 0)),
            pl.BlockSpec((1, 1, D_MODEL), vec),
            pl.BlockSpec((1, 1, D_MODEL), vec),
            _const_spec((1, 1, D_MODEL)),
            _const_spec((1, 1, D_MODEL)),
            _const_spec((1, D_MODEL)),
            _const_spec((D_MODEL, IN_PACKED)),
            _const_spec((1, Q_LORA)),
            _const_spec((MLA_HEADS * HEAD_PAD, Q_LORA)),
            _const_spec((1, KV_LORA)),
            _const_spec((KV_LORA, MLA_HEADS * HEAD_PAD)),
            _const_spec((MLA_WIDTH, KV_LORA)),
            pl.BlockSpec((tm, HEAD_PAD), lambda b, i: (i, 0)),
            pl.BlockSpec((tm, HEAD_PAD), lambda b, i: (i, 0)),
            pl.BlockSpec((HEAD_PAD, tm), lambda b, i: (0, i)),
            pl.BlockSpec((HEAD_PAD, tm), lambda b, i: (0, i)),
        ],
        out_specs=[
            pl.BlockSpec((1, MLA_HEADS * HEAD_PAD, tm), lambda b, i: (b, 0, i)),
            pl.BlockSpec((1, tm, MLA_HEADS * HEAD_PAD), tok),
            pl.BlockSpec((1, MLA_HEADS * V_ROWS, tm), lambda b, i: (b, 0, i)),
            pl.BlockSpec((1, tm, cw), tok),
        ],
        out_shape=[
            jax.ShapeDtypeStruct((bsz, MLA_HEADS * HEAD_PAD, n_all), BF16),
            jax.ShapeDtypeStruct((bsz, n_all, MLA_HEADS * HEAD_PAD), BF16),
            jax.ShapeDtypeStruct((bsz, MLA_HEADS * V_ROWS, n_all), BF16),
            jax.ShapeDtypeStruct((bsz, n_all, cw), F32),
        ],
        compiler_params=pltpu.CompilerParams(
            dimension_semantics=("arbitrary", "arbitrary"), vmem_limit_bytes=VMEM_LIMIT),
        name="in_proj",
    )(xl, xc, mod_l[0], mod_l[1], mod_c[0], mod_c[1], wts["g_pre_mix"], wts["w_in"], wts["g_q"],
      wts["w_q_t"], wts["g_kv"], wts["w_k"], wts["w_vt"], cos_t, sin_t, cos_t.T, sin_t.T)


def _attn_kernel(qt_ref, k_ref, vt_ref, o_ref, sa_ref, sb_ref, acc_ref, *, tq, tk):
    n_h = qt_ref.shape[1] // HEAD_PAD
    n_q = qt_ref.shape[2] // tq
    n_c = k_ref.shape[1] // tk
    total = n_h * n_q * n_c
    acc_ref[...] = jnp.zeros_like(acc_ref)

    def advance(pos):
        hd, qb, ck = pos
        wrap_c = ck == n_c - 1
        wrap_q = jnp.logical_and(wrap_c, qb == n_q - 1)
        return (jnp.where(wrap_q, hd + 1, hd),
                jnp.where(wrap_q, 0, jnp.where(wrap_c, qb + 1, qb)),
                jnp.where(wrap_c, 0, ck + 1))

    def scores(pos, s_ref):
        hd, qb, ck = pos
        h0 = pl.multiple_of(hd * HEAD_PAD, HEAD_PAD)
        qt = qt_ref[0, pl.ds(h0, HEAD_PAD), pl.ds(pl.multiple_of(qb * tq, tq), tq)]
        kc = k_ref[0, pl.ds(pl.multiple_of(ck * tk, tk), tk), pl.ds(h0, HEAD_PAD)]
        s = jnp.dot(kc, qt, preferred_element_type=F32)
        s_ref[:, 0:tq] = s
        return jnp.max(s, axis=0, keepdims=True)

    def accumulate(pos, s_ref, s_max, m):
        hd, qb, ck = pos
        m_prev = jnp.where(ck == 0, NEG_BIG, m)
        m_new = jnp.maximum(m_prev, s_max)
        alpha = jnp.exp2(m_prev - m_new)
        p = jnp.exp2(s_ref[:, 0:tq] - m_new).astype(BF16)
        vc = vt_ref[0, pl.ds(pl.multiple_of(hd * V_ROWS, BF16_ROWS), V_ROWS),
                    pl.ds(pl.multiple_of(ck * tk, tk), tk)]
        acc = acc_ref[...] * alpha + jnp.dot(vc, p, preferred_element_type=F32)
        acc_ref[...] = acc
        o_ref[0, pl.ds(pl.multiple_of(hd * V_DIM, SUBLANES), V_DIM),
              pl.ds(pl.multiple_of(qb * tq, tq), tq)] = acc[0:V_DIM] * (1.0 / acc[V_DIM:V_DIM + 1])
        return m_new

    zero = jnp.int32(0)
    pos = (zero, zero, zero)
    m = jnp.full((1, tq), NEG_BIG, F32)
    max_a = scores(pos, sa_ref)

    def body(_, carry):
        h0, q0, c0, max_a, m = carry
        pos0 = (h0, q0, c0)
        pos1 = advance(pos0)
        pos2 = advance(pos1)
        max_b = scores(pos1, sb_ref)
        m = accumulate(pos0, sa_ref, max_a, m)
        max_a = scores(pos2, sa_ref)
        m = accumulate(pos1, sb_ref, max_b, m)
        return pos2 + (max_a, m)

    n_pairs = (total - 1) // 2
    if n_pairs > 0:
        carry = lax.fori_loop(0, n_pairs, body, pos + (max_a, m), unroll=PAIR_UNROLL)
        pos, max_a, m = carry[0:3], carry[3], carry[4]
    if total % 2 == 0:
        pos1 = advance(pos)
        max_b = scores(pos1, sb_ref)
        m = accumulate(pos, sa_ref, max_a, m)
        accumulate(pos1, sb_ref, max_b, m)
    else:
        accumulate(pos, sa_ref, max_a, m)


def _attention(q, k, vt, seq, t, blk, tq, tk):
    bsz = q.shape[0]
    return pl.pallas_call(
        functools.partial(_attn_kernel, tq=tq, tk=tk),
        grid=(bsz, MLA_HEADS // HEADS_PER_STEP),
        in_specs=[
            pl.BlockSpec((1, HEADS_PER_STEP * HEAD_PAD, seq), lambda b, h: (b, h, blk)),
            pl.BlockSpec((1, t, HEADS_PER_STEP * HEAD_PAD), lambda b, h: (b, blk, h)),
            pl.BlockSpec((1, HEADS_PER_STEP * V_ROWS, t), lambda b, h: (b, h, blk)),
        ],
        out_specs=pl.BlockSpec((1, HEADS_PER_STEP * V_DIM, seq), lambda b, h: (b, h, 0)),
        out_shape=jax.ShapeDtypeStruct((bsz, MLA_WIDTH, seq), F32),
        scratch_shapes=[
            pltpu.VMEM((tk, tq + SCORE_ROW_PAD), F32),
            pltpu.VMEM((tk, tq + SCORE_ROW_PAD), F32),
            pltpu.VMEM((V_ROWS, tq), F32),
        ],
        compiler_params=pltpu.CompilerParams(
            dimension_semantics=("arbitrary", "arbitrary"), vmem_limit_bytes=VMEM_LIMIT),
        name="attention",
    )(q, k, vt)


def _mix_tile(tile, n_tiles, x_ref, ot_ref, cv_ref, prev_ref, next_ref, gate_ref, gbr_ref, wo_ref,
              cw_ref, cb_ref, lng_ref, lnb_ref, sw_ref, gpost_ref, uext_ref, ush_ref, cext_ref):
    tm = x_ref.shape[1]
    has_prev = jnp.where(tile > 0, 1.0, 0.0)
    has_next = jnp.where(tile < n_tiles - 1, 1.0, 0.0)
    c0, c1, c2 = 0, CONF_WIDTH, CONF_WIDTH + SC_WIDTH

    a = ot_ref[0].T
    an = (a * _rms_scale(a, MLA_WIDTH)) * gbr_ref[:, 0:MLA_WIDTH]
    y = jnp.dot(an.astype(BF16), wo_ref[0:MLA_WIDTH, :], preferred_element_type=F32)

    yield
    uext_ref[0:HALO, :] = prev_ref[0, :, c0:c1] * has_prev
    uext_ref[HALO:HALO + tm, :] = cv_ref[0, :, c0:c1]
    uext_ref[HALO + tm:, :] = next_ref[0, :, c0:c1] * has_next
    base = HALO - CONF_K // 2
    n_a = ush_ref.shape[1] // SUBLANES - tm // SUBLANES + 1
    conv = cb_ref[...]
    for r in range(SUBLANES):
        if r == SUBLANES // 2:
            yield
        ush_ref[r] = uext_ref[base + r:base + r + ush_ref.shape[1], :]
        for a_i in range(n_a):
            t = SUBLANES * a_i + r
            if t < CONF_K:
                conv = conv + cw_ref[t:t + 1, :] * ush_ref[r, SUBLANES * a_i:SUBLANES * a_i + tm, :]
    yield
    mu = jnp.sum(conv, axis=-1, keepdims=True) * (1.0 / CONF_WIDTH)
    d = conv - mu
    ln = (d * _rms_scale(d, CONF_WIDTH)) * lng_ref[...] + lnb_ref[...]
    cf = ln * jax.nn.sigmoid(ln)
    cfn = (cf * _rms_scale(cf, CONF_WIDTH)) * gbr_ref[:, MLA_WIDTH:MLA_WIDTH + CONF_WIDTH]
    y = y + jnp.dot(cfn.astype(BF16), wo_ref[MLA_WIDTH:MLA_WIDTH + CONF_WIDTH, :],
                    preferred_element_type=F32)

    yield
    cext_ref[0:SUBLANES, :] = prev_ref[0, HALO - SUBLANES:, c1:c2] * has_prev
    cext_ref[SUBLANES:SUBLANES + tm, :] = cv_ref[0, :, c1:c2]
    cext_ref[SUBLANES + tm:, :] = next_ref[0, 0:SUBLANES, c1:c2] * has_next
    base = SUBLANES - SC_K // 2
    sc = sw_ref[0:1, :] * cext_ref[base:base + tm, :]
    for t in range(1, SC_K):
        sc = sc + sw_ref[t:t + 1, :] * cext_ref[base + t:base + t + tm, :]
    sc = cv_ref[0, :, c2:] * sc
    scn = (sc * _rms_scale(sc, SC_WIDTH)) * gbr_ref[:, MLA_WIDTH + CONF_WIDTH:]
    y = y + jnp.dot(scn.astype(BF16), wo_ref[MLA_WIDTH + CONF_WIDTH:, :],
                    preferred_element_type=F32)

    yn = (y * _rms_scale(y, D_MODEL)) * gpost_ref[...]
    return x_ref[0] + gate_ref[0] * yn


def _mlp_tile(x, shift_ref, scale_ref, gate_ref, gpre_ref, w1_ref, w2_ref, gpost_ref, ff_chunk):
    gain = gpre_ref[...] * (1.0 + scale_ref[0])
    h = ((x * _rms_scale(x, D_MODEL)) * gain + shift_ref[0]).astype(BF16)
    y = None
    for j in range(D_FF // ff_chunk):
        yield
        lo = j * ff_chunk
        a = jnp.maximum(jnp.dot(h, w1_ref[:, lo:lo + ff_chunk], preferred_element_type=F32), 0.0)
        part = jnp.dot((a * a).astype(BF16), w2_ref[lo:lo + ff_chunk, :],
                       preferred_element_type=F32)
        y = part if y is None else y + part
    yn = (y * _rms_scale(y, D_MODEL)) * gpost_ref[...]
    return x + gate_ref[0] * yn


def _alternate(*stage_generators):
    pending = dict(enumerate(stage_generators))
    results = {}
    while pending:
        for key in list(pending):
            try:
                next(pending[key])
            except StopIteration as done:
                results[key] = done.value
                del pending[key]
    return tuple(results[k] for k in range(len(stage_generators)))


def _post_kernel(x_ref, ot_ref, cv_ref, prev_ref, next_ref, gmix_ref, gbr_ref, wo_ref, cw_ref,
                 cb_ref, lng_ref, lnb_ref, sw_ref, gpmix_ref, shift_ref, scale_ref, gmlp_ref,
                 gpre_ref, w1_ref, w2_ref, gpmlp_ref, o_ref,
                 uext_ref, ush_ref, cext_ref, new_ref, old_ref, *, n_tiles, ff_chunk):
    j = pl.program_id(1)
    if n_tiles == 1:
        (mixed,) = _alternate(_mix_tile(
            0, 1, x_ref, ot_ref, cv_ref, prev_ref, next_ref, gmix_ref, gbr_ref, wo_ref, cw_ref,
            cb_ref, lng_ref, lnb_ref, sw_ref, gpmix_ref, uext_ref, ush_ref, cext_ref))
        (o_ref[0],) = _alternate(_mlp_tile(mixed, shift_ref, scale_ref, gmlp_ref, gpre_ref,
                                           w1_ref, w2_ref, gpmlp_ref, ff_chunk))
        return

    @pl.when(j == 0)
    def _():
        old_ref[...] = jnp.zeros_like(old_ref)

    mix = _mix_tile(jnp.minimum(j, n_tiles - 1), n_tiles, x_ref, ot_ref, cv_ref, prev_ref,
                    next_ref, gmix_ref, gbr_ref, wo_ref, cw_ref, cb_ref, lng_ref, lnb_ref, sw_ref,
                    gpmix_ref, uext_ref, ush_ref, cext_ref)
    mlp = _mlp_tile(old_ref[...], shift_ref, scale_ref, gmlp_ref, gpre_ref, w1_ref, w2_ref,
                    gpmlp_ref, ff_chunk)
    mixed, out = _alternate(mix, mlp)
    new_ref[...] = mixed
    o_ref[0] = out
    old_ref[...] = new_ref[...]


def _post(x, ot, cv, row0, mods, wts, tm, ff_chunk=1024):
    bsz, seq, _ = x.shape
    n_t = seq // tm
    hb = tm // HALO
    t0, h0 = row0 // tm, row0 // HALO
    n_h = cv.shape[1] // HALO
    cw = cv.shape[-1]
    lag = 1 if n_t > 1 else 0
    tile = lambda i: jnp.minimum(i, n_t - 1)
    vec = lambda b, i: (b, 0, 0)
    single = lambda shape: pl.BlockSpec(shape, lambda b, i: (0,) * len(shape),
                                        pipeline_mode=pl.Buffered(1))
    return pl.pallas_call(
        functools.partial(_post_kernel, n_tiles=n_t, ff_chunk=ff_chunk),
        grid=(bsz, n_t + lag),
        in_specs=[
            pl.BlockSpec((1, tm, D_MODEL), lambda b, i: (b, tile(i), 0)),
            pl.BlockSpec((1, MLA_WIDTH, tm), lambda b, i: (b, 0, tile(i))),
            pl.BlockSpec((1, tm, cw), lambda b, i: (b, tile(i) + t0, 0)),
            pl.BlockSpec((1, HALO, cw),
                         lambda b, i: (b, jnp.maximum(h0 + tile(i) * hb - 1, 0), 0)),
            pl.BlockSpec((1, HALO, cw),
                         lambda b, i: (b, jnp.minimum(h0 + (tile(i) + 1) * hb, n_h - 1), 0)),
            pl.BlockSpec((1, 1, D_MODEL), vec),
            _const_spec((1, D_MODEL)),
            single((D_MODEL, D_MODEL)),
            _const_spec((CONF_K, CONF_WIDTH)),
            _const_spec((1, CONF_WIDTH)),
            _const_spec((1, CONF_WIDTH)),
            _const_spec((1, CONF_WIDTH)),
            _const_spec((SC_K, SC_WIDTH)),
            _const_spec((1, D_MODEL)),
            pl.BlockSpec((1, 1, D_MODEL), vec),
            pl.BlockSpec((1, 1, D_MODEL), vec),
            pl.BlockSpec((1, 1, D_MODEL), vec),
            _const_spec((1, D_MODEL)),
            single((D_MODEL, D_FF)),
            single((D_FF, D_MODEL)),
            _const_spec((1, D_MODEL)),
        ],
        out_specs=pl.BlockSpec((1, tm, D_MODEL), lambda b, i: (b, jnp.maximum(i - lag, 0), 0)),
        out_shape=jax.ShapeDtypeStruct((bsz, seq, D_MODEL), F32),
        scratch_shapes=[
            pltpu.VMEM((tm + 2 * HALO, CONF_WIDTH), F32),
            pltpu.VMEM((SUBLANES, tm + SUBLANES * (pl.cdiv(CONF_K, SUBLANES) - 1), CONF_WIDTH), F32),
            pltpu.VMEM((tm + 2 * SUBLANES, SC_WIDTH), F32),
            pltpu.VMEM((tm, D_MODEL), F32),
            pltpu.VMEM((tm, D_MODEL), F32),
        ],
        compiler_params=pltpu.CompilerParams(
            dimension_semantics=("arbitrary", "arbitrary"), vmem_limit_bytes=VMEM_LIMIT),
        name="post",
    )(x, ot, cv, cv, cv, mods[0], wts["g_branch"], wts["w_o"], wts["conf_dw_w"], wts["conf_dw_b"],
      wts["conf_ln_g"], wts["conf_ln_b"], wts["sc_dw_w"], wts["g_post_mix"], mods[1], mods[2],
      mods[3], wts["g_pre_mlp"], wts["w_mlp_in"], wts["w_mlp_out"], wts["g_post_mlp"])


def _rotate_half_cols(w):
    hh = AXIS_DIM // 2
    parts = []
    for ax in range(2):
        blk = w[..., ax * AXIS_DIM:(ax + 1) * AXIS_DIM]
        parts.append(jnp.concatenate([-blk[..., hh:], blk[..., :hh]], axis=-1))
    return jnp.concatenate(parts, axis=-1)


def _head_group(nope, rope):
    pad = jnp.zeros(nope.shape[:-1] + (HEAD_PAD - QK_DIM,), nope.dtype)
    return jnp.concatenate([nope, rope, pad], axis=-1)


def _pack_layer(i, w_in, g_q, w_q_b, g_kv, w_kv_b, p):
    win = w_in[i]
    zeros_nope = jnp.zeros((D_MODEL, QK_NOPE), F32)
    w_kr = win[:, Q_LORA + KV_LORA:Q_LORA + KV_LORA + QK_ROPE]
    conf0 = Q_LORA + KV_LORA + QK_ROPE
    sc0 = conf0 + 2 * CONF_WIDTH
    win_packed = jnp.concatenate([
        win[:, 0:Q_LORA],
        win[:, Q_LORA:Q_LORA + KV_LORA],
        jnp.concatenate([zeros_nope, w_kr, _rotate_half_cols(w_kr)], axis=-1),
        win[:, conf0:sc0],
        win[:, sc0:],
    ], axis=-1)

    wq = w_q_b[i].reshape(Q_LORA, MLA_HEADS, QK_DIM)
    wq_nope, wq_rope = wq[..., :QK_NOPE], wq[..., QK_NOPE:]
    wq_main = _head_group(wq_nope, wq_rope).reshape(Q_LORA, MLA_HEADS * HEAD_PAD)
    wq_rot = _head_group(jnp.zeros_like(wq_nope), _rotate_half_cols(wq_rope)).reshape(
        Q_LORA, MLA_HEADS * HEAD_PAD)

    wkv = w_kv_b[i].reshape(KV_LORA, MLA_HEADS, QK_NOPE + V_DIM)
    wk = _head_group(wkv[..., :QK_NOPE], jnp.zeros((KV_LORA, MLA_HEADS, QK_ROPE), F32)).reshape(
        KV_LORA, MLA_HEADS * HEAD_PAD)
    wv_t = jnp.transpose(wkv[..., QK_NOPE:], (1, 2, 0))

    row = lambda v: v[i][None, :]
    return {
        "g_pre_mix": row(p["g_pre_mix"]),
        "w_in": win_packed.astype(BF16),
        "g_q": row(g_q) * (SM_SCALE * math.log2(math.e)),
        "w_q_t": jnp.concatenate([wq_nope, wq_rope, _rotate_half_cols(wq_rope)], axis=-1).reshape(
            Q_LORA, MLA_HEADS * HEAD_PAD).T.astype(BF16),
        "g_kv": row(g_kv),
        "w_k": wk.astype(BF16),
        "w_vt": wv_t.reshape(MLA_WIDTH, KV_LORA).astype(BF16),
        "g_branch": row(p["g_branch"]),
        "w_o": p["w_o"][i].astype(BF16),
        "conf_dw_w": p["conf_dw_w"][i],
        "conf_dw_b": row(p["conf_dw_b"]),
        "conf_ln_g": row(p["conf_ln_g"]),
        "conf_ln_b": row(p["conf_ln_b"]),
        "sc_dw_w": p["sc_dw_w"][i],
        "g_post_mix": row(p["g_post_mix"]),
        "g_pre_mlp": row(p["g_pre_mlp"]),
        "w_mlp_in": p["w_mlp_in"][i].astype(BF16),
        "w_mlp_out": p["w_mlp_out"][i].astype(BF16),
        "g_post_mlp": row(p["g_post_mlp"]),
    }


def _rope_tables(n_lat):
    rows = n_lat // GRID_W
    row = jnp.broadcast_to(jnp.arange(rows)[:, None], (rows, GRID_W)).reshape(-1).astype(F32)
    col = jnp.broadcast_to(jnp.arange(GRID_W)[None, :], (rows, GRID_W)).reshape(-1).astype(F32)
    inv = 1.0 / (ROPE_THETA ** (jnp.arange(0, AXIS_DIM, 2, dtype=F32) / AXIS_DIM))
    ar = row[:, None] * inv
    ac = col[:, None] * inv
    ang = jnp.concatenate([ar, ar, ac, ac], axis=-1)
    return (_head_group(jnp.ones((n_lat, QK_NOPE), F32), jnp.cos(ang)),
            _head_group(jnp.zeros((n_lat, QK_NOPE), F32), jnp.sin(ang)))


def _identity_tables(n):
    return (_head_group(jnp.ones((n, QK_NOPE), F32), jnp.ones((n, QK_ROPE), F32)),
            jnp.zeros((n, HEAD_PAD), F32))


def kernel(x, c, ctx, c_ctx, w_mod, b_mod, g_pre_mix, g_post_mix, g_pre_mlp, g_post_mlp, w_in,
           g_q, w_q_b, g_kv, w_kv_b, conf_dw_w, conf_dw_b, conf_ln_g, conf_ln_b, sc_dw_w,
           g_branch, w_o, w_mlp_in, w_mlp_out):
    bsz, n_lat, _ = x.shape
    n_ctx = ctx.shape[1]
    depth = w_mod.shape[0]
    params = dict(g_pre_mix=g_pre_mix, g_post_mix=g_post_mix, g_pre_mlp=g_pre_mlp,
                  g_post_mlp=g_post_mlp, conf_dw_w=conf_dw_w, conf_dw_b=conf_dw_b,
                  conf_ln_g=conf_ln_g, conf_ln_b=conf_ln_b, sc_dw_w=sc_dw_w, g_branch=g_branch,
                  w_o=w_o, w_mlp_in=w_mlp_in, w_mlp_out=w_mlp_out)

    mod_rows = SUBLANES * pl.cdiv(bsz + 1, SUBLANES)
    cvecs = jnp.concatenate(
        [c, c_ctx[None, :], jnp.zeros((mod_rows - bsz - 1, D_MODEL), F32)], axis=0)
    mod = _modulation(cvecs, w_mod, b_mod).reshape(depth, mod_rows, N_MOD, D_MODEL)

    cos_l, sin_l = _rope_tables(n_lat)
    cos_c, sin_c = _identity_tables(n_ctx)
    cos_t = jnp.concatenate([cos_l, cos_c], axis=0)
    sin_t = jnp.concatenate([sin_l, sin_c], axis=0)
    n_all = n_lat + n_ctx
    tm = min(256, n_ctx)
    tq_lat, tk_lat = 512, 768
    assert n_lat % tm == 0 and n_ctx % tm == 0 and n_lat % n_ctx == 0
    tm_post = 256
    assert n_lat % tq_lat == 0 and n_all % tk_lat == 0 and n_lat % tm_post == 0
    ctx_blk = n_lat // n_ctx

    xl, xc = x, ctx
    for i in range(depth):
        last = i == depth - 1
        wts = _pack_layer(i, w_in, g_q, w_q_b, g_kv, w_kv_b, params)
        ml = [mod[i, :bsz, j][:, None, :] for j in range(N_MOD)]
        mc1 = [mod[i, bsz, j][None, None, :] for j in range(N_MOD)]

        q, k, vt, cv = _in_proj(xl, xc, ml[0:2], mc1[0:2], wts, cos_t, sin_t, tm)

        ot_l = _attention(q, k, vt, n_lat, n_all, 0, tq_lat, tk_lat)
        xl = _post(xl, ot_l, cv, 0, ml[2:6], wts, tm_post)

        if not last:
            mc = [jnp.broadcast_to(v, (bsz, 1, D_MODEL)) for v in mc1]
            ot_c = _attention(q, k, vt, n_ctx, n_ctx, ctx_blk, n_ctx, n_ctx)
            xc = _post(xc, ot_c, cv, n_lat, mc[2:6], wts, tm)
    return xl
```
